```python
import jax
import jax.numpy as jnp
from jax import lax
import numpy as np

D_MODEL = 2048
BATCH = 2
SEQ = 8192
DEPTH = 1
DEC_BATCH = 32
DEC_SEQ = 4
PAST_LEN = 16384
PAGE_SIZE = 128

HEAD_DIM = 128
N_HEADS_MOBA = 8
N_HEADS_SB = 8
W_MOBA = N_HEADS_MOBA * HEAD_DIM
W_SB = N_HEADS_SB * HEAD_DIM
MOBA_BLOCK = 256
MOBA_TOPK = 3
MOBA_Q_CHUNK = 32
SB_Q_BLOCK = 128
D_FF = 4 * D_MODEL
ROPE_THETA = 10000.0
NORM_EPS = 1e-6
N_MOD = 6
PROJ_SPLITS = [W_MOBA, 2 * W_MOBA, 3 * W_MOBA, 3 * W_MOBA + W_SB, 3 * W_MOBA + 2 * W_SB,
               3 * W_MOBA + 3 * W_SB, 3 * W_MOBA + 3 * W_SB + D_MODEL]
PROJ_COLS = 3 * W_MOBA + 3 * W_SB + 2 * D_MODEL

kernel_name = "hybrid_moba_stickbreak_decoder_step"


def _rms(x, g):
    xf = x.astype(jnp.float32)
    y = xf * lax.rsqrt(jnp.mean(xf * xf, axis=-1, keepdims=True) + NORM_EPS)
    return (y * g.astype(jnp.float32)).astype(x.dtype)


def _rope(x, pos):
    half = HEAD_DIM // 2
    inv = ROPE_THETA ** (-jnp.arange(half, dtype=jnp.float32) / half)
    ang = pos.astype(jnp.float32)[:, None] * inv[None, :]
    cos = jnp.cos(ang)[None, :, None, :]
    sin = jnp.sin(ang)[None, :, None, :]
    xf = x.astype(jnp.float32)
    x1, x2 = xf[..., :half], xf[..., half:]
    return jnp.concatenate([x1 * cos - x2 * sin, x2 * cos + x1 * sin], axis=-1).astype(x.dtype)


def _stick_break(z, valid, carry):
    z = z.astype(jnp.float32)
    log_1m = jnp.where(valid, jax.nn.log_sigmoid(-z), 0.0)
    suffix = lax.cumsum(log_1m, axis=z.ndim - 1, reverse=True) - log_1m + carry[..., None]
    a = jnp.where(valid, jnp.exp(jax.nn.log_sigmoid(z) + suffix), 0.0)
    return a, jnp.sum(log_1m, axis=-1)


def _sb_prompt(q, k, v):
    B, S, H, _ = q.shape
    nq = S // SB_Q_BLOCK
    scale = HEAD_DIM ** -0.5
    kpos = jnp.arange(S)
    qb = q.reshape(B, nq, SB_Q_BLOCK, H, HEAD_DIM).transpose(1, 0, 2, 3, 4)

    def block(args):
        i, q_blk = args
        qpos = i * SB_Q_BLOCK + jnp.arange(SB_Q_BLOCK)
        z = jnp.einsum('bqhd,bkhd->bhqk', q_blk, k).astype(jnp.float32) * scale
        valid = kpos[None, :] < qpos[:, None]
        a, _ = _stick_break(z, valid, jnp.zeros(z.shape[:-1], jnp.float32))
        return jnp.einsum('bhqk,bkhd->bqhd', a, v)

    o = lax.map(block, (jnp.arange(nq), qb))
    return o.transpose(1, 0, 2, 3, 4).reshape(B, S, H, HEAD_DIM).astype(q.dtype)


def _sb_sample(q, k, v, cache_k, cache_v, page_table, layer):
    DB, T, H, _ = q.shape
    scale = HEAD_DIM ** -0.5
    i = jnp.arange(T)
    z_new = jnp.einsum('bqhd,bkhd->bhqk', q, k).astype(jnp.float32) * scale
    a_new, carry0 = _stick_break(z_new, i[None, :] < i[:, None], jnp.zeros((DB, H, T), jnp.float32))
    out0 = jnp.einsum('bhqk,bkhd->bhqd', a_new, v.astype(jnp.float32))

    def page_step(carry, pt):
        acc, suffix = carry
        kp = cache_k[layer, pt]
        vp = cache_v[layer, pt]
        z = jnp.einsum('bqhd,bkhd->bhqk', q, kp).astype(jnp.float32) * scale
        a, tot = _stick_break(z, True, suffix)
        acc = acc + jnp.einsum('bhqk,bkhd->bhqd', a, vp.astype(jnp.float32))
        return (acc, suffix + tot), None

    (acc, _), _ = lax.scan(page_step, (out0, carry0), page_table.T, reverse=True)
    return acc.transpose(0, 2, 1, 3).astype(q.dtype)


def _moba_prompt(q, k, v):
    B, S, H, _ = q.shape
    nb = -(-S // MOBA_BLOCK)
    s_pad = nb * MOBA_BLOCK - S
    scale = HEAD_DIM ** -0.5
    n_sel = min(MOBA_TOPK, nb)
    pad = ((0, 0), (0, s_pad), (0, 0), (0, 0))
    kb = jnp.pad(k, pad).reshape(B, nb, MOBA_BLOCK, H, HEAD_DIM).transpose(0, 3, 1, 2, 4)
    vb = jnp.pad(v, pad).reshape(B, nb, MOBA_BLOCK, H, HEAD_DIM).transpose(0, 3, 1, 2, 4)
    k_mean = jnp.mean(kb.astype(jnp.float32), axis=3)
    nc = S // MOBA_Q_CHUNK
    qc = q.reshape(B, nc, MOBA_Q_CHUNK, H, HEAD_DIM).transpose(1, 0, 3, 2, 4)
    bi = jnp.arange(B)[:, None, None, None]
    hi = jnp.arange(H)[None, :, None, None]
    blk_ids = jnp.arange(nb)

    def chunk(args):
        ci, q_c = args
        qpos = ci * MOBA_Q_CHUNK + jnp.arange(MOBA_Q_CHUNK)
        own = (ci * MOBA_Q_CHUNK) // MOBA_BLOCK
        gate = jnp.einsum('bhqd,bhnd->bhqn', q_c.astype(jnp.float32), k_mean)
        gate = jnp.where(blk_ids < own, gate, -jnp.inf)
        _, top_i = lax.top_k(gate, n_sel)
        sel_ok = top_i < own
        k_sel = kb[bi, hi, top_i]
        v_sel = vb[bi, hi, top_i]
        s_sel = jnp.einsum('bhqd,bhqnkd->bhqnk', q_c, k_sel).astype(jnp.float32) * scale
        s_sel = jnp.where(sel_ok[..., None], s_sel, -jnp.inf)
        k_own = lax.dynamic_index_in_dim(kb, own, axis=2, keepdims=False)
        v_own = lax.dynamic_index_in_dim(vb, own, axis=2, keepdims=False)
        s_own = jnp.einsum('bhqd,bhkd->bhqk', q_c, k_own).astype(jnp.float32) * scale
        kpos = own * MOBA_BLOCK + jnp.arange(MOBA_BLOCK)
        s_own = jnp.where(kpos[None, :] <= qpos[:, None], s_own, -jnp.inf)
        n_rows = n_sel * MOBA_BLOCK
        w = jax.nn.softmax(jnp.concatenate([s_sel.reshape(B, H, MOBA_Q_CHUNK, n_rows), s_own], axis=-1), axis=-1)
        w_sel = w[..., :n_rows].reshape(B, H, MOBA_Q_CHUNK, n_sel, MOBA_BLOCK)
        return (jnp.einsum('bhqnk,bhqnkd->bqhd', w_sel, v_sel)
                + jnp.einsum('bhqk,bhkd->bqhd', w[..., n_rows:], v_own))

    o = lax.map(chunk, (jnp.arange(nc), qc))
    return o.transpose(1, 0, 2, 3, 4).reshape(B, S, H, HEAD_DIM).astype(q.dtype)


def _moba_sample(q, k, v, cache_k, cache_v, page_table, layer):
    DB, T, H, _ = q.shape
    n_pages = page_table.shape[1]
    past = n_pages * PAGE_SIZE
    ppb = MOBA_BLOCK // PAGE_SIZE
    n_full = past // MOBA_BLOCK
    own_start = n_full * MOBA_BLOCK
    n_own_past = past - own_start
    scale = HEAD_DIM ** -0.5
    own_pages = page_table[:, own_start // PAGE_SIZE:]
    k_own = jnp.concatenate([cache_k[layer, own_pages].reshape(DB, n_own_past, H, HEAD_DIM), k], axis=1)
    v_own = jnp.concatenate([cache_v[layer, own_pages].reshape(DB, n_own_past, H, HEAD_DIM), v], axis=1)
    s_own = jnp.einsum('bqhd,bkhd->bhqk', q, k_own).astype(jnp.float32) * scale
    qi = jnp.arange(T)
    kj = jnp.arange(n_own_past + T)
    s_own = jnp.where(kj[None, :] <= n_own_past + qi[:, None], s_own, -jnp.inf)
    if n_full == 0:
        w = jax.nn.softmax(s_own, axis=-1)
        return jnp.einsum('bhqk,bkhd->bqhd', w, v_own).astype(q.dtype)
    page_sums = lax.map(lambda pt: jnp.sum(cache_k[layer, pt].astype(jnp.float32), axis=1),
                        page_table[:, :n_full * ppb].T)
    k_mean = page_sums.reshape(n_full, ppb, DB, H, HEAD_DIM).sum(axis=1) / MOBA_BLOCK
    gate = jnp.einsum('bqhd,nbhd->bhqn', q.astype(jnp.float32), k_mean)
    n_sel = min(MOBA_TOPK, n_full)
    _, top_i = lax.top_k(gate, n_sel)
    bi = jnp.arange(DB)[:, None, None, None, None]
    phys = page_table[bi, top_i[..., None] * ppb + jnp.arange(ppb)]
    hi = jnp.arange(H)[None, :, None, None, None, None]
    rows = jnp.arange(PAGE_SIZE)
    n_rows = n_sel * MOBA_BLOCK
    k_sel = cache_k[layer, phys[..., None], rows, hi].reshape(DB, H, T, n_rows, HEAD_DIM)
    v_sel = cache_v[layer, phys[..., None], rows, hi].reshape(DB, H, T, n_rows, HEAD_DIM)
    s_sel = jnp.einsum('bqhd,bhqkd->bhqk', q, k_sel).astype(jnp.float32) * scale
    w = jax.nn.softmax(jnp.concatenate([s_sel, s_own], axis=-1), axis=-1)
    o = (jnp.einsum('bhqk,bhqkd->bqhd', w[..., :n_rows], v_sel)
         + jnp.einsum('bhqk,bkhd->bqhd', w[..., n_rows:], v_own))
    return o.astype(q.dtype)


def _attn_inputs(x, c, pos, w_ada, b_ada, attn_norm_g, w_in, q_norm_g, k_norm_g):
    B, T, _ = x.shape
    mods = jnp.split((c @ w_ada + b_ada)[:, None, :], N_MOD, axis=-1)
    h = _rms(x, attn_norm_g) * (1.0 + mods[1]) + mods[0]
    q_m, k_m, v_m, q_s, k_s, v_s, g_m, g_s = jnp.split(h @ w_in, PROJ_SPLITS, axis=-1)
    q_m = _rope(_rms(q_m.reshape(B, T, N_HEADS_MOBA, HEAD_DIM), q_norm_g), pos)
    k_m = _rope(_rms(k_m.reshape(B, T, N_HEADS_MOBA, HEAD_DIM), k_norm_g), pos)
    v_m = v_m.reshape(B, T, N_HEADS_MOBA, HEAD_DIM)
    q_s = q_s.reshape(B, T, N_HEADS_SB, HEAD_DIM)
    k_s = k_s.reshape(B, T, N_HEADS_SB, HEAD_DIM)
    v_s = v_s.reshape(B, T, N_HEADS_SB, HEAD_DIM)
    return mods, q_m, k_m, v_m, q_s, k_s, v_s, jax.nn.sigmoid(g_m), jax.nn.sigmoid(g_s)


def _merge_and_ffn(x, mods, o_m, o_s, g_m, g_s, w_br_moba, w_br_sb, w_out, mlp_norm_g, w_ff1, w_ff2):
    B, T, _ = x.shape
    u = g_m * (o_m.reshape(B, T, W_MOBA) @ w_br_moba) + g_s * (o_s.reshape(B, T, W_SB) @ w_br_sb)
    x = x + mods[2] * (u @ w_out)
    h = _rms(x, mlp_norm_g) * (1.0 + mods[4]) + mods[3]
    return x + mods[5] * (jnp.square(jax.nn.relu(h @ w_ff1)) @ w_ff2)


def setup_inputs(seed: int = 0) -> dict:
    key = jax.random.key(seed)
    ks = jax.random.split(key, 24)
    n_pages = PAST_LEN // PAGE_SIZE
    n_used = DEC_BATCH * n_pages
    n_phys = n_used + -(-n_used // 4)
    f32 = jnp.float32

    def nrm(k, shape, s=1.0):
        return s * jax.random.normal(k, shape, f32)

    cshape_m = (DEPTH, n_phys, PAGE_SIZE, N_HEADS_MOBA, HEAD_DIM)
    cshape_s = (DEPTH, n_phys, PAGE_SIZE, N_HEADS_SB, HEAD_DIM)
    page_table = jax.random.permutation(ks[6], n_phys)[:n_used].reshape(DEC_BATCH, n_pages).astype(jnp.int32)
    return {
        "x_prompt": nrm(ks[0], (BATCH, SEQ, D_MODEL)),
        "x_sample": nrm(ks[1], (DEC_BATCH, DEC_SEQ, D_MODEL)),
        "cache_k_moba": nrm(ks[2], cshape_m),
        "cache_v_moba": nrm(ks[3], cshape_m),
        "cache_k_sb": nrm(ks[4], cshape_s),
        "cache_v_sb": nrm(ks[5], cshape_s),
        "page_table": page_table,
        "c_prompt": nrm(ks[7], (BATCH, D_MODEL)),
        "c_sample": nrm(ks[8], (DEC_BATCH, D_MODEL)),
        "w_ada": nrm(ks[9], (DEPTH, D_MODEL, N_MOD * D_MODEL), 0.5 * D_MODEL ** -0.5),
        "b_ada": nrm(ks[10], (DEPTH, N_MOD * D_MODEL), 0.01),
        "attn_norm_g": 1.0 + nrm(ks[11], (DEPTH, D_MODEL), 0.01),
        "w_in": nrm(ks[12], (DEPTH, D_MODEL, PROJ_COLS), D_MODEL ** -0.5),
        "q_norm_g": 1.0 + nrm(ks[13], (DEPTH, HEAD_DIM), 0.01),
        "k_norm_g": 1.0 + nrm(ks[14], (DEPTH, HEAD_DIM), 0.01),
        "w_br_moba": nrm(ks[15], (DEPTH, W_MOBA, D_MODEL), W_MOBA ** -0.5),
        "w_br_sb": nrm(ks[16], (DEPTH, W_SB, D_MODEL), W_SB ** -0.5),
        "w_out": nrm(ks[17], (DEPTH, D_MODEL, D_MODEL), D_MODEL ** -0.5),
        "mlp_norm_g": 1.0 + nrm(ks[18], (DEPTH, D_MODEL), 0.01),
        "w_ff1": nrm(ks[19], (DEPTH, D_MODEL, D_FF), D_MODEL ** -0.5),
        "w_ff2": nrm(ks[20], (DEPTH, D_FF, D_MODEL), D_FF ** -0.5),
    }


def reference(x_prompt, x_sample, cache_k_moba, cache_v_moba, cache_k_sb, cache_v_sb, page_table,
              c_prompt, c_sample, w_ada, b_ada, attn_norm_g, w_in, q_norm_g, k_norm_g,
              w_br_moba, w_br_sb, w_out, mlp_norm_g, w_ff1, w_ff2):
    past_len = page_table.shape[1] * PAGE_SIZE
    pos_p = jnp.arange(x_prompt.shape[1], dtype=jnp.int32)
    pos_s = past_len + jnp.arange(x_sample.shape[1], dtype=jnp.int32)
    xp, xs = x_prompt, x_sample
    kmp, vmp, ksp, vsp, kms, vms, kss, vss = [], [], [], [], [], [], [], []
    for l in range(DEPTH):
        pre_w = (w_ada[l], b_ada[l], attn_norm_g[l], w_in[l], q_norm_g[l], k_norm_g[l])
        post_w = (w_br_moba[l], w_br_sb[l], w_out[l], mlp_norm_g[l], w_ff1[l], w_ff2[l])
        mods, qm, km, vm, qs, ks_, vs, gm, gs = _attn_inputs(xp, c_prompt, pos_p, *pre_w)
        om = _moba_prompt(qm, km, vm)
        osb = _sb_prompt(qs, ks_, vs)
        xp = _merge_and_ffn(xp, mods, om, osb, gm, gs, *post_w)
        kmp.append(km); vmp.append(vm); ksp.append(ks_); vsp.append(vs)
        mods, qm, km, vm, qs, ks_, vs, gm, gs = _attn_inputs(xs, c_sample, pos_s, *pre_w)
        om = _moba_sample(qm, km, vm, cache_k_moba, cache_v_moba, page_table, l)
        osb = _sb_sample(qs, ks_, vs, cache_k_sb, cache_v_sb, page_table, l)
        xs = _merge_and_ffn(xs, mods, om, osb, gm, gs, *post_w)
        kms.append(km); vms.append(vm); kss.append(ks_); vss.append(vs)
    new_k_moba_prompt = jnp.stack(kmp)
    new_v_moba_prompt = jnp.stack(vmp)
    new_k_sb_prompt = jnp.stack(ksp)
    new_v_sb_prompt = jnp.stack(vsp)
    new_k_moba_sample = jnp.stack(kms)
    new_v_moba_sample = jnp.stack(vms)
    new_k_sb_sample = jnp.stack(kss)
    new_v_sb_sample = jnp.stack(vss)
    return (xp, xs, new_k_moba_prompt, new_v_moba_prompt, new_k_sb_prompt, new_v_sb_prompt,
            new_k_moba_sample, new_v_moba_sample, new_k_sb_sample, new_v_sb_sample)
```

```python
import functools

import jax
import jax.numpy as jnp
from jax import lax
from jax.experimental import pallas as pl
from jax.experimental.pallas import tpu as pltpu

HEAD_DIM = 128
N_HEADS = 8
WIDTH = N_HEADS * HEAD_DIM
MOBA_BLOCK = 256
MOBA_TOPK = 3
PAGE_SIZE = 128
PAGES_PER_BLOCK = MOBA_BLOCK // PAGE_SIZE
ROPE_THETA = 10000.0
NORM_EPS = 1e-6
N_MOD = 6
ATTN_SCALE = HEAD_DIM ** -0.5

LANES = 128
SUBLANES = 8
VMEM_LIMIT_BYTES = 56 * 1024 * 1024
NEG_BIG = -1e30
SB_STOP = -120.0

F32 = jnp.float32
BF16 = jnp.bfloat16
NT_DIMS = (((1,), (1,)), ((), ()))


def _params(*sem):
    return pltpu.CompilerParams(dimension_semantics=sem, vmem_limit_bytes=VMEM_LIMIT_BYTES)


def _mods_kernel(c_ref, w_ref, b_ref, o_ref):
    o_ref[...] = jnp.dot(c_ref[...], w_ref[...], precision=lax.Precision.HIGHEST,
                         preferred_element_type=F32) + b_ref[...]


def _ada_mods(c, w, b):
    r, d = c.shape
    n = w.shape[1]
    tn = 512
    return pl.pallas_call(
        _mods_kernel,
        grid=(n // tn,),
        in_specs=[pl.BlockSpec((r, d), lambda j: (0, 0)),
                  pl.BlockSpec((d, tn), lambda j: (0, j)),
                  pl.BlockSpec((1, tn), lambda j: (0, j))],
        out_specs=pl.BlockSpec((r, tn), lambda j: (0, j)),
        out_shape=jax.ShapeDtypeStruct((r, n), F32),
        compiler_params=_params("arbitrary"),
        name="ada_mods",
    )(c, w, b.reshape(1, n))


def _norm_mod(x, gain, scale, shift):
    ms = jnp.mean(x * x, axis=-1, keepdims=True)
    return (x * lax.rsqrt(ms + NORM_EPS) * gain) * (1.0 + scale) + shift


def _norm_mod_kernel(x_ref, g_ref, sc_ref, sh_ref, o_ref):
    o_ref[...] = _norm_mod(x_ref[...], g_ref[...], sc_ref[0], sh_ref[0]).astype(o_ref.dtype)


def _mod_spec(mod, m, tm):
    g, r, d = mod.shape
    tiles_per_group = (m // g) // tm
    return pl.BlockSpec((1, r, d), lambda i, *_: (i // tiles_per_group, 0, 0))


def _norm_call(x, gain, scale, shift, tm):
    m, d = x.shape
    return pl.pallas_call(
        _norm_mod_kernel,
        grid=(m // tm,),
        in_specs=[pl.BlockSpec((tm, d), lambda i: (i, 0)),
                  pl.BlockSpec((1, d), lambda i: (0, 0)),
                  _mod_spec(scale, m, tm), _mod_spec(shift, m, tm)],
        out_specs=pl.BlockSpec((tm, d), lambda i: (i, 0)),
        out_shape=jax.ShapeDtypeStruct((m, d), BF16),
        compiler_params=_params("arbitrary"),
        name="norm_mod",
    )(x, gain.reshape(1, d), scale, shift)


def _proj_rope_kernel(h_ref, w_ref, g_ref, cos_ref, sin_ref, of_ref, ob_ref, *, out_scale):
    z = jnp.dot(h_ref[...], w_ref[...], preferred_element_type=F32)
    g = g_ref[...]
    cos = cos_ref[...]
    sin = sin_ref[...]
    for hd in range(z.shape[1] // HEAD_DIM):
        sl = slice(hd * HEAD_DIM, (hd + 1) * HEAD_DIM)
        zh = z[:, sl]
        y = zh * lax.rsqrt(jnp.mean(zh * zh, axis=-1, keepdims=True) + NORM_EPS) * g
        y = y * cos + pltpu.roll(y, HEAD_DIM // 2, 1) * sin
        of_ref[:, sl] = y
        ob_ref[:, sl] = (y * out_scale).astype(ob_ref.dtype)


def _proj_raw_kernel(h_ref, w_ref, of_ref, ob_ref, *, out_scale):
    z = jnp.dot(h_ref[...], w_ref[...], preferred_element_type=F32)
    of_ref[...] = z
    ob_ref[...] = (z * out_scale).astype(ob_ref.dtype)


def _proj_sigmoid_kernel(h_ref, w_ref, o_ref):
    z = jnp.dot(h_ref[...], w_ref[...], preferred_element_type=F32)
    o_ref[...] = 1.0 / (1.0 + jnp.exp(-z))


def _proj_call(kind, h, w, tm, *, out_scale=1.0, gain=None, cos=None, sin=None):
    m, d = h.shape
    n = w.shape[1]
    row = lambda i: (i, 0)
    h_spec = pl.BlockSpec((tm, d), row)
    w_spec = pl.BlockSpec((d, n), lambda i: (0, 0))
    o_spec = pl.BlockSpec((tm, n), row)
    two_out = dict(out_specs=[o_spec, o_spec],
                   out_shape=[jax.ShapeDtypeStruct((m, n), F32), jax.ShapeDtypeStruct((m, n), BF16)])
    if kind == "rope":
        tab_tiles = cos.shape[0] // tm
        tab_spec = pl.BlockSpec((tm, HEAD_DIM), lambda i: (i % tab_tiles, 0))
        return pl.pallas_call(
            functools.partial(_proj_rope_kernel, out_scale=out_scale),
            grid=(m // tm,),
            in_specs=[h_spec, w_spec, pl.BlockSpec((1, HEAD_DIM), lambda i: (0, 0)), tab_spec, tab_spec],
            compiler_params=_params("arbitrary"), name="proj_rope", **two_out,
        )(h, w, gain.reshape(1, HEAD_DIM), cos, sin)
    if kind == "raw":
        return pl.pallas_call(
            functools.partial(_proj_raw_kernel, out_scale=out_scale),
            grid=(m // tm,), in_specs=[h_spec, w_spec],
            compiler_params=_params("arbitrary"), name="proj_raw", **two_out,
        )(h, w)
    assert kind == "sigmoid"
    return pl.pallas_call(
        _proj_sigmoid_kernel,
        grid=(m // tm,), in_specs=[h_spec, w_spec], out_specs=o_spec,
        out_shape=jax.ShapeDtypeStruct((m, n), F32),
        compiler_params=_params("arbitrary"), name="proj_sigmoid",
    )(h, w)


def _merge_kernel(om_ref, os_ref, gm_ref, gs_ref, x_ref, wm_ref, ws_ref, wo_ref,
                  ga_ref, g2_ref, sc_ref, sh_ref, x1_ref, h2_ref):
    u = (gm_ref[...] * jnp.dot(om_ref[...], wm_ref[...], preferred_element_type=F32)
         + gs_ref[...] * jnp.dot(os_ref[...], ws_ref[...], preferred_element_type=F32))
    x1 = x_ref[...] + ga_ref[0] * jnp.dot(u.astype(BF16), wo_ref[...], preferred_element_type=F32)
    x1_ref[...] = x1
    h2_ref[...] = _norm_mod(x1, g2_ref[...], sc_ref[0], sh_ref[0]).astype(h2_ref.dtype)


def _merge_call(o_m, o_s, g_m, g_s, x, w_m, w_s, w_o, gate_a, gain2, scale_f, shift_f, tm):
    m, d = x.shape
    wd = o_m.shape[1]
    row = lambda i: (i, 0)
    const = lambda i: (0, 0)
    resident = functools.partial(pl.BlockSpec, index_map=const, pipeline_mode=pl.Buffered(1))
    return pl.pallas_call(
        _merge_kernel,
        grid=(m // tm,),
        in_specs=[pl.BlockSpec((tm, wd), row), pl.BlockSpec((tm, wd), row),
                  pl.BlockSpec((tm, d), row), pl.BlockSpec((tm, d), row), pl.BlockSpec((tm, d), row),
                  resident((wd, d)), resident((wd, d)), resident((d, d)),
                  _mod_spec(gate_a, m, tm), pl.BlockSpec((1, d), const),
                  _mod_spec(scale_f, m, tm), _mod_spec(shift_f, m, tm)],
        out_specs=[pl.BlockSpec((tm, d), row), pl.BlockSpec((tm, d), row)],
        out_shape=[jax.ShapeDtypeStruct((m, d), F32), jax.ShapeDtypeStruct((m, d), BF16)],
        compiler_params=_params("arbitrary"),
        name="merge",
    )(o_m, o_s, g_m, g_s, x, w_m, w_s, w_o, gate_a, gain2.reshape(1, d), scale_f, shift_f)


def _ffn_kernel(h_ref, w1_ref, w2_ref, x_ref, gf_ref, o_ref, acc_ref):
    k = pl.program_id(1)

    @pl.when(k == 0)
    def _():
        acc_ref[...] = jnp.zeros_like(acc_ref)

    a = jnp.dot(h_ref[...], w1_ref[...], preferred_element_type=F32)
    a = jnp.square(jnp.maximum(a, 0.0)).astype(BF16)
    acc_ref[...] += jnp.dot(a, w2_ref[...], preferred_element_type=F32)

    @pl.when(k == pl.num_programs(1) - 1)
    def _():
        o_ref[...] = x_ref[...] + gf_ref[0] * acc_ref[...]


def _ffn_call(h2, w1, w2, x1, gate_f, tm, tf):
    m, d = x1.shape
    f = w1.shape[1]
    return pl.pallas_call(
        _ffn_kernel,
        grid=(m // tm, f // tf),
        in_specs=[pl.BlockSpec((tm, d), lambda i, k: (i, 0)),
                  pl.BlockSpec((d, tf), lambda i, k: (0, k)),
                  pl.BlockSpec((tf, d), lambda i, k: (k, 0)),
                  pl.BlockSpec((tm, d), lambda i, k: (i, 0)),
                  _mod_spec(gate_f, m, tm)],
        out_specs=pl.BlockSpec((tm, d), lambda i, k: (i, 0)),
        out_shape=jax.ShapeDtypeStruct((m, d), F32),
        scratch_shapes=[pltpu.VMEM((tm, d), F32)],
        compiler_params=_params("arbitrary", "arbitrary"),
        name="ffn",
    )(h2, w1, w2, x1, gate_f)


def _top_k_indices(gate, k):
    col = lax.broadcasted_iota(jnp.int32, gate.shape, 1)
    n = gate.shape[1]
    g = gate
    out = []
    for _ in range(k):
        best = jnp.max(g, axis=-1, keepdims=True)
        first = jnp.min(jnp.where(g == best, col, n), axis=-1, keepdims=True)
        out.append((first, best))
        g = jnp.where(col == first, -jnp.inf, g)
    return out


def _top_k_mask(gate, n_valid, k):
    col = lax.broadcasted_iota(jnp.int32, gate.shape, 1)
    sel = jnp.zeros(gate.shape, jnp.bool_)
    for first, best in _top_k_indices(jnp.where(col < n_valid, gate, -jnp.inf), k):
        sel = jnp.logical_or(sel, jnp.logical_and(col == first, best > -jnp.inf))
    return sel


def _moba_prompt_kernel(qf_ref, qb_ref, kf_ref, kb_ref, vb_ref, o_ref, kmean_ref):
    qi = pl.program_id(2)
    blk = MOBA_BLOCK
    n_blocks = kmean_ref.shape[0]

    @pl.when(qi == 0)
    def _():
        for n in range(n_blocks):
            kmean_ref[n:n + 1, :] = jnp.sum(kf_ref[n * blk:(n + 1) * blk, :], axis=0, keepdims=True) * (1.0 / blk)

    gate = lax.dot_general(qf_ref[...], kmean_ref[...], NT_DIMS, precision=lax.Precision.HIGHEST,
                           preferred_element_type=F32)
    bias = jnp.where(_top_k_mask(gate, qi, MOBA_TOPK), 0.0, NEG_BIG)
    blk_col = lax.broadcasted_iota(jnp.int32, bias.shape, 1)

    q = qb_ref[...]
    own = pl.multiple_of(qi * blk, blk)
    s = lax.dot_general(q, kb_ref[pl.ds(own, blk), :], NT_DIMS, preferred_element_type=F32)
    r_id = lax.broadcasted_iota(jnp.int32, s.shape, 0)
    c_id = lax.broadcasted_iota(jnp.int32, s.shape, 1)
    s = jnp.where(c_id <= r_id, s, NEG_BIG)
    m0 = jnp.max(s, axis=-1, keepdims=True)
    p = jnp.exp(s - m0)
    l0 = jnp.sum(p, axis=-1, keepdims=True)
    acc0 = jnp.dot(p.astype(BF16), vb_ref[pl.ds(own, blk), :], preferred_element_type=F32)

    def past_block(n, carry):
        m, l, acc = carry
        start = pl.multiple_of(n * blk, blk)
        bias_n = jnp.sum(jnp.where(blk_col == n, bias, 0.0), axis=-1, keepdims=True)
        s = lax.dot_general(q, kb_ref[pl.ds(start, blk), :], NT_DIMS, preferred_element_type=F32) + bias_n
        m_new = jnp.maximum(m, jnp.max(s, axis=-1, keepdims=True))
        alpha = jnp.exp(m - m_new)
        p = jnp.exp(s - m_new)
        l = alpha * l + jnp.sum(p, axis=-1, keepdims=True)
        acc = alpha * acc + jnp.dot(p.astype(BF16), vb_ref[pl.ds(start, blk), :], preferred_element_type=F32)
        return m_new, l, acc

    _, l, acc = lax.fori_loop(0, qi, past_block, (m0, l0, acc0))
    o_ref[...] = (acc / l).astype(o_ref.dtype)


def _moba_prompt_call(qf, qb, kf, kb, vb, batch):
    m, _ = qf.shape
    s = m // batch
    assert s % MOBA_BLOCK == 0
    tq = MOBA_BLOCK
    nq = s // tq
    q_spec = pl.BlockSpec((tq, HEAD_DIM), lambda b, h, i: (b * nq + i, h))
    kv_spec = pl.BlockSpec((s, HEAD_DIM), lambda b, h, i: (b, h))
    return pl.pallas_call(
        _moba_prompt_kernel,
        grid=(batch, N_HEADS, nq),
        in_specs=[q_spec, q_spec, kv_spec, kv_spec, kv_spec],
        out_specs=q_spec,
        out_shape=jax.ShapeDtypeStruct((m, WIDTH), BF16),
        scratch_shapes=[pltpu.VMEM((s // MOBA_BLOCK, HEAD_DIM), F32)],
        compiler_params=_params("arbitrary", "arbitrary", "arbitrary"),
        name="moba_prompt",
    )(qf, qb, kf, kb, vb)


def _neg_softplus(z):
    return -(jnp.maximum(z, 0.0) + jnp.log(1.0 + jnp.exp(-jnp.abs(z))))


def _split_bf16(x):
    hi = x.astype(BF16)
    return hi, (x - hi.astype(F32)).astype(BF16)


def _sb_weights(z, carry, valid):
    tk = z.shape[1]
    log_1m = _neg_softplus(z)
    if valid is not None:
        log_1m = jnp.where(valid, log_1m, 0.0)
    j_id = lax.broadcasted_iota(jnp.int32, (tk, tk), 0)
    s_id = lax.broadcasted_iota(jnp.int32, (tk, tk), 1)
    after = jnp.where(j_id > s_id, 1.0, 0.0).astype(BF16)
    hi, lo = _split_bf16(log_1m)
    suffix = (jnp.dot(hi, after, preferred_element_type=F32) + jnp.dot(lo, after, preferred_element_type=F32))
    a = jnp.exp(z + log_1m + suffix + carry)
    if valid is not None:
        a = jnp.where(valid, a, 0.0)
    return a, carry + jnp.sum(log_1m, axis=-1, keepdims=True)


def _sb_prompt_kernel(q_ref, k_ref, v_ref, o_ref):
    qi = pl.program_id(2)
    tq = q_ref.shape[0]
    q = q_ref[...]

    def tile(start, carry, valid):
        z = lax.dot_general(q, k_ref[pl.ds(start, tq), :], NT_DIMS, preferred_element_type=F32)
        a, carry = _sb_weights(z, carry, valid)
        return jnp.dot(a.astype(BF16), v_ref[pl.ds(start, tq), :], preferred_element_type=F32), carry

    r_id = lax.broadcasted_iota(jnp.int32, (tq, tq), 0)
    c_id = lax.broadcasted_iota(jnp.int32, (tq, tq), 1)
    acc, carry = tile(pl.multiple_of(qi * tq, tq), jnp.zeros((tq, 1), F32), c_id < r_id)

    def more(state):
        j, _, carry = state
        return jnp.logical_and(j >= 0, jnp.max(carry) > SB_STOP)

    def step(state):
        j, acc, carry = state
        out, carry = tile(pl.multiple_of(j * tq, tq), carry, None)
        return j - 1, acc + out, carry

    _, acc, _ = lax.while_loop(more, step, (qi - 1, acc, carry))
    o_ref[...] = acc.astype(o_ref.dtype)


def _sb_prompt_call(qb, kb, vb, batch):
    m, _ = qb.shape
    s = m // batch
    tq = 128
    nq = s // tq
    q_spec = pl.BlockSpec((tq, HEAD_DIM), lambda b, h, i: (b * nq + i, h))
    kv_spec = pl.BlockSpec((s, HEAD_DIM), lambda b, h, i: (b, h))
    return pl.pallas_call(
        _sb_prompt_kernel,
        grid=(batch, N_HEADS, nq),
        in_specs=[q_spec, kv_spec, kv_spec],
        out_specs=q_spec,
        out_shape=jax.ShapeDtypeStruct((m, WIDTH), BF16),
        compiler_params=_params("arbitrary", "arbitrary", "arbitrary"),
        name="sb_prompt",
    )(qb, kb, vb)


def _sb_sample_kernel(pt_ref, q_ref, kn_ref, vn_ref, kp_ref, vp_ref, o_ref, acc_ref, carry_ref, *, n_tok):
    del pt_ref
    p = pl.program_id(1)
    rows = N_HEADS * SUBLANES

    def head_q(h):
        return q_ref[0, :, h * HEAD_DIM:(h + 1) * HEAD_DIM].astype(BF16)

    def tile(k_of_head, v_of_head, carry, valid):
        z = jnp.concatenate([lax.dot_general(head_q(h), k_of_head(h), NT_DIMS, preferred_element_type=F32)
                             for h in range(N_HEADS)], axis=0)
        a, carry = _sb_weights(z, carry, valid)
        a = a.astype(BF16)
        out = jnp.concatenate([jnp.dot(a[h * SUBLANES:(h + 1) * SUBLANES], v_of_head(h), preferred_element_type=F32)
                               for h in range(N_HEADS)], axis=0)
        return out, carry

    @pl.when(p == 0)
    def _():
        tk = kn_ref.shape[1]
        key = lax.broadcasted_iota(jnp.int32, (rows, tk), 1)
        tok = lax.broadcasted_iota(jnp.int32, (rows, tk), 0) % SUBLANES
        out, carry = tile(lambda h: kn_ref[0, :, h * HEAD_DIM:(h + 1) * HEAD_DIM],
                          lambda h: vn_ref[0, :, h * HEAD_DIM:(h + 1) * HEAD_DIM],
                          jnp.zeros((rows, 1), F32), key < tok)
        acc_ref[...] = out
        carry_ref[...] = carry

    tok_of_row = lax.broadcasted_iota(jnp.int32, (rows, 1), 0) % SUBLANES
    live = jnp.max(jnp.where(tok_of_row < n_tok, carry_ref[...], SB_STOP)) > SB_STOP

    @pl.when(live)
    def _():
        out, carry = tile(lambda h: kp_ref[0, :, h, :].astype(BF16), lambda h: vp_ref[0, :, h, :].astype(BF16),
                          carry_ref[...], None)
        acc_ref[...] += out
        carry_ref[...] = carry

    @pl.when(p == pl.num_programs(1) - 1)
    def _():
        for h in range(N_HEADS):
            o_ref[0, :, h * HEAD_DIM:(h + 1) * HEAD_DIM] = acc_ref[h * SUBLANES:(h + 1) * SUBLANES, :].astype(o_ref.dtype)


def _sb_sample_call(q_pad, k_new, v_new, cache_k, cache_v, page_table, n_tok):
    db, rows, w = q_pad.shape
    n_pages = page_table.shape[1]
    page = lambda b, p, pt: (pt[b * n_pages + (n_pages - 1 - p)], 0, 0, 0)
    per_b = lambda b, p, pt: (b, 0, 0)
    return pl.pallas_call(
        functools.partial(_sb_sample_kernel, n_tok=n_tok),
        grid_spec=pltpu.PrefetchScalarGridSpec(
            num_scalar_prefetch=1,
            grid=(db, n_pages),
            in_specs=[pl.BlockSpec((1, rows, w), per_b),
                      pl.BlockSpec((1, PAGE_SIZE, w), per_b), pl.BlockSpec((1, PAGE_SIZE, w), per_b),
                      pl.BlockSpec((1, PAGE_SIZE, N_HEADS, HEAD_DIM), page),
                      pl.BlockSpec((1, PAGE_SIZE, N_HEADS, HEAD_DIM), page)],
            out_specs=pl.BlockSpec((1, rows, w), per_b),
            scratch_shapes=[pltpu.VMEM((N_HEADS * rows, HEAD_DIM), F32), pltpu.VMEM((N_HEADS * rows, 1), F32)]),
        out_shape=jax.ShapeDtypeStruct((db, rows, w), BF16),
        compiler_params=_params("arbitrary", "arbitrary"),
        name="sb_sample",
    )(page_table.reshape(-1), q_pad, k_new, v_new, cache_k, cache_v)


PAGES_PER_STEP = 16


def _page_sum_kernel(pt_ref, *refs):
    del pt_ref
    o_ref = refs[-1]
    for r in range(o_ref.shape[1]):
        tot = jnp.sum(refs[r * PAGES_PER_BLOCK][0], axis=0)
        for j in range(1, PAGES_PER_BLOCK):
            tot = tot + jnp.sum(refs[r * PAGES_PER_BLOCK + j][0], axis=0)
        o_ref[0, r] = tot


def _block_sums_call(cache_k, page_table):
    db, n_pages = page_table.shape
    pps = min(PAGES_PER_STEP, n_pages)
    assert n_pages % pps == 0

    def page(j):
        return lambda b, g, pt: (pt[b * n_pages + g * pps + j], 0, 0, 0)

    return pl.pallas_call(
        _page_sum_kernel,
        grid_spec=pltpu.PrefetchScalarGridSpec(
            num_scalar_prefetch=1,
            grid=(db, n_pages // pps),
            in_specs=[pl.BlockSpec((1, PAGE_SIZE, N_HEADS, HEAD_DIM), page(j)) for j in range(pps)],
            out_specs=pl.BlockSpec((1, pps // PAGES_PER_BLOCK, N_HEADS, HEAD_DIM), lambda b, g, pt: (b, g, 0, 0))),
        out_shape=jax.ShapeDtypeStruct((db, n_pages // PAGES_PER_BLOCK, N_HEADS, HEAD_DIM), F32),
        compiler_params=_params("arbitrary", "arbitrary"),
        name="moba_block_sums",
    )(page_table.reshape(-1), *([cache_k] * pps))


def _moba_select_kernel(bsum_ref, q_ref, o_ref):
    q = q_ref[0]
    gates = [lax.dot_general(q[:, h * HEAD_DIM:(h + 1) * HEAD_DIM], bsum_ref[0, :, h, :] * (1.0 / MOBA_BLOCK),
                             NT_DIMS, precision=lax.Precision.HIGHEST, preferred_element_type=F32)
             for h in range(N_HEADS)]
    gate = jnp.concatenate(gates, axis=0)
    lane = lax.broadcasted_iota(jnp.int32, o_ref.shape[1:], 1)
    out = jnp.zeros(o_ref.shape[1:], jnp.int32)
    for r, (idx, _) in enumerate(_top_k_indices(gate, MOBA_TOPK)):
        out = jnp.where(lane == r, idx, out)
    o_ref[0] = out


def _moba_select_call(block_sums, q_pad):
    db, n_blocks = block_sums.shape[:2]
    rows, w = q_pad.shape[1:]
    return pl.pallas_call(
        _moba_select_kernel,
        grid=(db,),
        in_specs=[pl.BlockSpec((1, n_blocks, N_HEADS, HEAD_DIM), lambda b: (b, 0, 0, 0)),
                  pl.BlockSpec((1, rows, w), lambda b: (b, 0, 0))],
        out_specs=pl.BlockSpec((1, N_HEADS * rows, LANES), lambda b: (b, 0, 0)),
        out_shape=jax.ShapeDtypeStruct((db, N_HEADS * rows, LANES), jnp.int32),
        compiler_params=_params("arbitrary"),
        name="moba_select",
    )(block_sums, q_pad)


def _moba_sample_kernel(sel_ref, pt_ref, q_ref, kn_ref, vn_ref, ck_ref, cv_ref, o_ref, kbuf, vbuf, sem,
                        *, n_tok, n_pages):
    step = pl.program_id(0)
    n_steps = pl.num_programs(0)
    per_tok = MOBA_TOPK * PAGES_PER_BLOCK

    def copies(s, slot):
        b = s // N_HEADS
        h = s % N_HEADS
        out = []
        for t in range(n_tok):
            for r in range(MOBA_TOPK):
                blk = sel_ref[(s * n_tok + t) * MOBA_TOPK + r]
                for j in range(PAGES_PER_BLOCK):
                    phys = pt_ref[b * n_pages + blk * PAGES_PER_BLOCK + j]
                    i = (t * MOBA_TOPK + r) * PAGES_PER_BLOCK + j
                    out.append(pltpu.make_async_copy(ck_ref.at[phys, :, h, :], kbuf.at[slot, i], sem.at[slot, 0]))
                    out.append(pltpu.make_async_copy(cv_ref.at[phys, :, h, :], vbuf.at[slot, i], sem.at[slot, 1]))
        return out

    slot = step % 2

    @pl.when(step == 0)
    def _():
        for c in copies(step, slot):
            c.start()

    @pl.when(step + 1 < n_steps)
    def _():
        for c in copies(step + 1, 1 - slot):
            c.start()

    for c in copies(step, slot):
        c.wait()

    q = q_ref[0].astype(BF16)
    row = lax.broadcasted_iota(jnp.int32, (SUBLANES, 1), 0)
    s_sel = jnp.zeros((SUBLANES, per_tok * PAGE_SIZE), F32)
    for t in range(n_tok):
        k_t = kbuf[slot, t * per_tok:(t + 1) * per_tok].reshape(per_tok * PAGE_SIZE, HEAD_DIM).astype(BF16)
        s_sel = jnp.where(row == t, lax.dot_general(q, k_t, NT_DIMS, preferred_element_type=F32), s_sel)
    s_own = lax.dot_general(q, kn_ref[0].astype(BF16), NT_DIMS, preferred_element_type=F32)
    key = lax.broadcasted_iota(jnp.int32, s_own.shape, 1)
    s_own = jnp.where(key <= row, s_own, NEG_BIG)
    m = jnp.maximum(jnp.max(s_sel, axis=-1, keepdims=True), jnp.max(s_own, axis=-1, keepdims=True))
    p_sel = jnp.exp(s_sel - m)
    p_own = jnp.exp(s_own - m)
    denom = jnp.sum(p_sel, axis=-1, keepdims=True) + jnp.sum(p_own, axis=-1, keepdims=True)
    out = jnp.dot(p_own.astype(BF16), vn_ref[0].astype(BF16), preferred_element_type=F32)
    p_sel = p_sel.astype(BF16)
    for t in range(n_tok):
        v_t = vbuf[slot, t * per_tok:(t + 1) * per_tok].reshape(per_tok * PAGE_SIZE, HEAD_DIM).astype(BF16)
        out = out + jnp.where(row == t, jnp.dot(p_sel, v_t, preferred_element_type=F32), 0.0)
    o_ref[0] = (out / denom).astype(o_ref.dtype)


def _moba_sample_call(sel, page_table, q_pad, k_pad, v_pad, cache_k, cache_v, n_tok):
    db, n_pages = page_table.shape
    rows, w = q_pad.shape[1:]
    n_sel = n_tok * MOBA_TOPK * PAGES_PER_BLOCK
    new_spec = pl.BlockSpec((1, rows, HEAD_DIM), lambda s, *_: (s // N_HEADS, 0, s % N_HEADS))
    any_spec = pl.BlockSpec(memory_space=pl.ANY)
    return pl.pallas_call(
        functools.partial(_moba_sample_kernel, n_tok=n_tok, n_pages=n_pages),
        grid_spec=pltpu.PrefetchScalarGridSpec(
            num_scalar_prefetch=2,
            grid=(db * N_HEADS,),
            in_specs=[new_spec, new_spec, new_spec, any_spec, any_spec],
            out_specs=new_spec,
            scratch_shapes=[pltpu.VMEM((2, n_sel, PAGE_SIZE, HEAD_DIM), F32),
                            pltpu.VMEM((2, n_sel, PAGE_SIZE, HEAD_DIM), F32),
                            pltpu.SemaphoreType.DMA((2, 2))]),
        out_shape=jax.ShapeDtypeStruct((db, rows, w), BF16),
        compiler_params=_params("arbitrary"),
        name="moba_sample",
    )(sel.reshape(-1), page_table.reshape(-1), q_pad, k_pad, v_pad, cache_k, cache_v)


def _rope_tables(pos):
    half = HEAD_DIM // 2
    inv = ROPE_THETA ** (-jnp.arange(half, dtype=F32) / half)
    ang = pos.astype(F32)[:, None] * inv[None, :]
    cos, sin = jnp.cos(ang), jnp.sin(ang)
    return jnp.concatenate([cos, cos], axis=-1), jnp.concatenate([-sin, sin], axis=-1)


def _split_cols(w, bounds):
    return [w[:, a:b] for a, b in zip(bounds[:-1], bounds[1:])]


def _attn_inputs(x, mods, gain, w_parts, q_gain, k_gain, cos, sin, tm):
    shift_a, scale_a = mods
    h = _norm_call(x, gain, scale_a, shift_a, tm)
    w_qm, w_km, w_vm, w_qs, w_ks, w_vs, w_gm, w_gs = w_parts
    qm_f, qm_b = _proj_call("rope", h, w_qm, tm, out_scale=ATTN_SCALE, gain=q_gain, cos=cos, sin=sin)
    km_f, km_b = _proj_call("rope", h, w_km, tm, gain=k_gain, cos=cos, sin=sin)
    vm_f, vm_b = _proj_call("raw", h, w_vm, tm)
    qs_f, qs_b = _proj_call("raw", h, w_qs, tm, out_scale=ATTN_SCALE)
    ks_f, ks_b = _proj_call("raw", h, w_ks, tm)
    vs_f, vs_b = _proj_call("raw", h, w_vs, tm)
    g_m = _proj_call("sigmoid", h, w_gm, tm)
    g_s = _proj_call("sigmoid", h, w_gs, tm)
    return dict(qm_f=qm_f, qm_b=qm_b, km_f=km_f, km_b=km_b, vm_f=vm_f, vm_b=vm_b,
                qs_f=qs_f, qs_b=qs_b, ks_f=ks_f, ks_b=ks_b, vs_f=vs_f, vs_b=vs_b, g_m=g_m, g_s=g_s)


def kernel(x_prompt, x_sample, cache_k_moba, cache_v_moba, cache_k_sb, cache_v_sb, page_table, c_prompt, c_sample, w_ada, b_ada, attn_norm_g, w_in, q_norm_g, k_norm_g, w_br_moba, w_br_sb, w_out, mlp_norm_g, w_ff1, w_ff2):
    batch, seq, d = x_prompt.shape
    db, n_tok, _ = x_sample.shape
    depth, n_phys = cache_k_moba.shape[:2]
    n_pages = page_table.shape[1]
    past_len = n_pages * PAGE_SIZE
    assert past_len % MOBA_BLOCK == 0 and past_len // MOBA_BLOCK >= MOBA_TOPK and n_tok <= SUBLANES
    m_p, m_s = batch * seq, db * n_tok
    tm_p, tm_s = 512, m_s
    assert m_s % SUBLANES == 0 and seq % tm_p == 0

    cos_p, sin_p = _rope_tables(jnp.arange(seq, dtype=jnp.int32))
    cos_s, sin_s = _rope_tables(past_len + jnp.arange(n_tok, dtype=jnp.int32))
    cos_s, sin_s = jnp.tile(cos_s, (db, 1)), jnp.tile(sin_s, (db, 1))

    n_c = batch + db
    c_rows = -(-n_c // SUBLANES) * SUBLANES
    c_all = jnp.pad(jnp.concatenate([c_prompt, c_sample], axis=0), ((0, c_rows - n_c), (0, 0)))
    bounds = [0, WIDTH, 2 * WIDTH, 3 * WIDTH, 4 * WIDTH, 5 * WIDTH, 6 * WIDTH, 6 * WIDTH + d, 6 * WIDTH + 2 * d]
    cache_shape = (depth * n_phys, PAGE_SIZE, N_HEADS, HEAD_DIM)
    ck_m, cv_m = cache_k_moba.reshape(cache_shape), cache_v_moba.reshape(cache_shape)
    ck_s, cv_s = cache_k_sb.reshape(cache_shape), cache_v_sb.reshape(cache_shape)

    def pad_rows(v, rows):
        return jnp.pad(v.reshape(db, n_tok, WIDTH), ((0, 0), (0, rows - n_tok), (0, 0)))

    xp = x_prompt.reshape(m_p, d)
    xs = x_sample.reshape(m_s, d)
    outs = [[] for _ in range(8)]
    for l in range(depth):
        mods = _ada_mods(c_all, w_ada[l], b_ada[l])
        mods_p = [mods[:batch, i * d:(i + 1) * d].reshape(batch, 1, d) for i in range(N_MOD)]
        mods_s = [jnp.repeat(mods[batch:n_c, i * d:(i + 1) * d], n_tok, axis=0).reshape(1, m_s, d)
                  for i in range(N_MOD)]
        w_parts = _split_cols(w_in[l].astype(BF16), bounds)
        w_m, w_s, w_o = w_br_moba[l].astype(BF16), w_br_sb[l].astype(BF16), w_out[l].astype(BF16)
        w_1, w_2 = w_ff1[l].astype(BF16), w_ff2[l].astype(BF16)

        a = _attn_inputs(xp, mods_p[:2], attn_norm_g[l], w_parts, q_norm_g[l], k_norm_g[l], cos_p, sin_p, tm_p)
        o_m = _moba_prompt_call(a["qm_f"], a["qm_b"], a["km_f"], a["km_b"], a["vm_b"], batch)
        o_s = _sb_prompt_call(a["qs_b"], a["ks_b"], a["vs_b"], batch)
        x1, h2 = _merge_call(o_m, o_s, a["g_m"], a["g_s"], xp, w_m, w_s, w_o, mods_p[2], mlp_norm_g[l],
                             mods_p[4], mods_p[3], 256)
        xp = _ffn_call(h2, w_1, w_2, x1, mods_p[5], tm_p, 1024)
        for dst, key in zip(outs[:4], ("km_f", "vm_f", "ks_f", "vs_f")):
            dst.append(a[key].reshape(batch, seq, N_HEADS, HEAD_DIM))

        a = _attn_inputs(xs, mods_s[:2], attn_norm_g[l], w_parts, q_norm_g[l], k_norm_g[l], cos_s, sin_s, tm_s)
        pt = page_table + l * n_phys
        qm_pad = pad_rows(a["qm_f"], SUBLANES)
        block_sums = _block_sums_call(ck_m, pt)
        picks = _moba_select_call(block_sums, qm_pad)
        sel = picks.reshape(db, N_HEADS, SUBLANES, LANES)[:, :, :n_tok, :MOBA_TOPK]
        o_m = _moba_sample_call(sel, pt, qm_pad * ATTN_SCALE, pad_rows(a["km_f"], SUBLANES),
                                pad_rows(a["vm_f"], SUBLANES), ck_m, cv_m, n_tok)
        o_m = o_m[:, :n_tok].reshape(m_s, WIDTH)
        o_s = _sb_sample_call(pad_rows(a["qs_f"], SUBLANES) * ATTN_SCALE, pad_rows(a["ks_b"], PAGE_SIZE),
                              pad_rows(a["vs_b"], PAGE_SIZE), ck_s, cv_s, pt, n_tok)
        o_s = o_s[:, :n_tok].reshape(m_s, WIDTH)
        x1, h2 = _merge_call(o_m, o_s, a["g_m"], a["g_s"], xs, w_m, w_s, w_o, mods_s[2], mlp_norm_g[l],
                             mods_s[4], mods_s[3], tm_s)
        xs = _ffn_call(h2, w_1, w_2, x1, mods_s[5], tm_s, 1024)
        for dst, key in zip(outs[4:], ("km_f", "vm_f", "ks_f", "vs_f")):
            dst.append(a[key].reshape(db, n_tok, N_HEADS, HEAD_DIM))

    return (xp.reshape(batch, seq, d), xs.reshape(db, n_tok, d), *[jnp.stack(o) for o in outs])
```

```python
import functools

import jax
import jax.numpy as jnp
from jax import lax
from jax.experimental import pallas as pl
from jax.experimental.pallas import tpu as pltpu

HEAD_DIM = 128
N_HEADS = 8
WIDTH = N_HEADS * HEAD_DIM
MOBA_BLOCK = 256
MOBA_TOPK = 3
PAGE_SIZE = 128
PAGES_PER_BLOCK = MOBA_BLOCK // PAGE_SIZE
ROPE_THETA = 10000.0
NORM_EPS = 1e-6
N_MOD = 6
ATTN_SCALE = HEAD_DIM ** -0.5
LOG2E = 1.4426950408889634

LANES = 128
SUBLANES = 8
VMEM_LIMIT_BYTES = 56 * 1024 * 1024
NEG_BIG = -1e30
SB_STOP = -120.0 * LOG2E

F32 = jnp.float32
BF16 = jnp.bfloat16
NT_DIMS = (((1,), (1,)), ((), ()))


def _params(*sem):
    return pltpu.CompilerParams(dimension_semantics=sem, vmem_limit_bytes=VMEM_LIMIT_BYTES)


def _mods_kernel(c_ref, w_ref, b_ref, o_ref):
    o_ref[...] = jnp.dot(c_ref[...], w_ref[...], precision=lax.Precision.HIGHEST,
                         preferred_element_type=F32) + b_ref[...]


def _ada_mods(c, w, b):
    r, d = c.shape
    n = w.shape[1]
    tn = 512
    return pl.pallas_call(
        _mods_kernel,
        grid=(n // tn,),
        in_specs=[pl.BlockSpec((r, d), lambda j: (0, 0)),
                  pl.BlockSpec((d, tn), lambda j: (0, j)),
                  pl.BlockSpec((1, tn), lambda j: (0, j))],
        out_specs=pl.BlockSpec((r, tn), lambda j: (0, j)),
        out_shape=jax.ShapeDtypeStruct((r, n), F32),
        compiler_params=_params("arbitrary"),
        name="ada_mods",
    )(c, w, b.reshape(1, n))


def _norm_mod(x, gain, scale, shift):
    ms = jnp.mean(x * x, axis=-1, keepdims=True)
    return (x * lax.rsqrt(ms + NORM_EPS) * gain) * (1.0 + scale) + shift


def _norm_mod_kernel(x_ref, g_ref, sc_ref, sh_ref, o_ref):
    o_ref[...] = _norm_mod(x_ref[...], g_ref[...], sc_ref[0], sh_ref[0]).astype(o_ref.dtype)


def _mod_spec(mod, m, tm):
    g, r, d = mod.shape
    tiles_per_group = (m // g) // tm
    return pl.BlockSpec((1, r, d), lambda i, *_: (i // tiles_per_group, 0, 0))


def _norm_call(x, gain, scale, shift, tm):
    m, d = x.shape
    return pl.pallas_call(
        _norm_mod_kernel,
        grid=(m // tm,),
        in_specs=[pl.BlockSpec((tm, d), lambda i: (i, 0)),
                  pl.BlockSpec((1, d), lambda i: (0, 0)),
                  _mod_spec(scale, m, tm), _mod_spec(shift, m, tm)],
        out_specs=pl.BlockSpec((tm, d), lambda i: (i, 0)),
        out_shape=jax.ShapeDtypeStruct((m, d), BF16),
        compiler_params=_params("arbitrary"),
        name="norm_mod",
    )(x, gain.reshape(1, d), scale, shift)


def _proj_rope_kernel(h_ref, w_ref, g_ref, cos_ref, sin_ref, of_ref, ob_ref, *, out_scale):
    z = jnp.dot(h_ref[...], w_ref[...], preferred_element_type=F32)
    g = g_ref[...]
    cos = cos_ref[...]
    sin = sin_ref[...]
    for hd in range(z.shape[1] // HEAD_DIM):
        sl = slice(hd * HEAD_DIM, (hd + 1) * HEAD_DIM)
        zh = z[:, sl]
        y = zh * lax.rsqrt(jnp.mean(zh * zh, axis=-1, keepdims=True) + NORM_EPS) * g
        y = y * cos + pltpu.roll(y, HEAD_DIM // 2, 1) * sin
        of_ref[:, sl] = y
        ob_ref[:, sl] = (y * out_scale).astype(ob_ref.dtype)


def _proj_raw_kernel(h_ref, w_ref, *o_refs, out_scale):
    z = jnp.dot(h_ref[...], w_ref[...], preferred_element_type=F32)
    for o_ref in o_refs:
        o_ref[...] = z if o_ref.dtype == F32 else (z * out_scale).astype(o_ref.dtype)


def _proj_sigmoid_kernel(h_ref, w_ref, o_ref):
    z = jnp.dot(h_ref[...], w_ref[...], preferred_element_type=F32)
    o_ref[...] = 1.0 / (1.0 + jnp.exp(-z))


def _proj_call(kind, h, w, tm, *, out_scale=1.0, gain=None, cos=None, sin=None, dtypes=(F32, BF16)):
    m, d = h.shape
    n = w.shape[1]
    row = lambda i: (i, 0)
    h_spec = pl.BlockSpec((tm, d), row)
    w_spec = pl.BlockSpec((d, n), lambda i: (0, 0))
    o_spec = pl.BlockSpec((tm, n), row)
    two_out = dict(out_specs=[o_spec, o_spec],
                   out_shape=[jax.ShapeDtypeStruct((m, n), F32), jax.ShapeDtypeStruct((m, n), BF16)])
    if kind == "rope":
        tab_tiles = cos.shape[0] // tm
        tab_spec = pl.BlockSpec((tm, HEAD_DIM), lambda i: (i % tab_tiles, 0))
        return pl.pallas_call(
            functools.partial(_proj_rope_kernel, out_scale=out_scale),
            grid=(m // tm,),
            in_specs=[h_spec, w_spec, pl.BlockSpec((1, HEAD_DIM), lambda i: (0, 0)), tab_spec, tab_spec],
            compiler_params=_params("arbitrary"), name="proj_rope", **two_out,
        )(h, w, gain.reshape(1, HEAD_DIM), cos, sin)
    if kind == "raw":
        return pl.pallas_call(
            functools.partial(_proj_raw_kernel, out_scale=out_scale),
            grid=(m // tm,), in_specs=[h_spec, w_spec],
            out_specs=[o_spec] * len(dtypes),
            out_shape=[jax.ShapeDtypeStruct((m, n), dt) for dt in dtypes],
            compiler_params=_params("arbitrary"), name="proj_raw",
        )(h, w)
    assert kind == "sigmoid"
    return pl.pallas_call(
        _proj_sigmoid_kernel,
        grid=(m // tm,), in_specs=[h_spec, w_spec], out_specs=o_spec,
        out_shape=jax.ShapeDtypeStruct((m, n), F32),
        compiler_params=_params("arbitrary"), name="proj_sigmoid",
    )(h, w)


def _merge_kernel(om_ref, os_ref, gm_ref, gs_ref, x_ref, wm_ref, ws_ref, wo_ref,
                  ga_ref, g2_ref, sc_ref, sh_ref, x1_ref, h2_ref):
    u = (gm_ref[...] * jnp.dot(om_ref[...], wm_ref[...], preferred_element_type=F32)
         + gs_ref[...] * jnp.dot(os_ref[...], ws_ref[...], preferred_element_type=F32))
    x1 = x_ref[...] + ga_ref[0] * jnp.dot(u.astype(BF16), wo_ref[...], preferred_element_type=F32)
    x1_ref[...] = x1
    h2_ref[...] = _norm_mod(x1, g2_ref[...], sc_ref[0], sh_ref[0]).astype(h2_ref.dtype)


def _merge_call(o_m, o_s, g_m, g_s, x, w_m, w_s, w_o, gate_a, gain2, scale_f, shift_f, tm):
    m, d = x.shape
    wd = o_m.shape[1]
    row = lambda i: (i, 0)
    const = lambda i: (0, 0)
    resident = functools.partial(pl.BlockSpec, index_map=const, pipeline_mode=pl.Buffered(1))
    return pl.pallas_call(
        _merge_kernel,
        grid=(m // tm,),
        in_specs=[pl.BlockSpec((tm, wd), row), pl.BlockSpec((tm, wd), row),
                  pl.BlockSpec((tm, d), row), pl.BlockSpec((tm, d), row), pl.BlockSpec((tm, d), row),
                  resident((wd, d)), resident((wd, d)), resident((d, d)),
                  _mod_spec(gate_a, m, tm), pl.BlockSpec((1, d), const),
                  _mod_spec(scale_f, m, tm), _mod_spec(shift_f, m, tm)],
        out_specs=[pl.BlockSpec((tm, d), row), pl.BlockSpec((tm, d), row)],
        out_shape=[jax.ShapeDtypeStruct((m, d), F32), jax.ShapeDtypeStruct((m, d), BF16)],
        compiler_params=_params("arbitrary"),
        name="merge",
    )(o_m, o_s, g_m, g_s, x, w_m, w_s, w_o, gate_a, gain2.reshape(1, d), scale_f, shift_f)


def _ffn_kernel(h_ref, w1_ref, w2_ref, x_ref, gf_ref, o_ref, acc_ref):
    k = pl.program_id(1)

    @pl.when(k == 0)
    def _():
        acc_ref[...] = jnp.zeros_like(acc_ref)

    a = jnp.dot(h_ref[...], w1_ref[...], preferred_element_type=F32)
    a = jnp.square(jnp.maximum(a, 0.0)).astype(BF16)
    acc_ref[...] += jnp.dot(a, w2_ref[...], preferred_element_type=F32)

    @pl.when(k == pl.num_programs(1) - 1)
    def _():
        o_ref[...] = x_ref[...] + gf_ref[0] * acc_ref[...]


def _ffn_call(h2, w1, w2, x1, gate_f, tm, tf):
    m, d = x1.shape
    f = w1.shape[1]
    return pl.pallas_call(
        _ffn_kernel,
        grid=(m // tm, f // tf),
        in_specs=[pl.BlockSpec((tm, d), lambda i, k: (i, 0)),
                  pl.BlockSpec((d, tf), lambda i, k: (0, k)),
                  pl.BlockSpec((tf, d), lambda i, k: (k, 0)),
                  pl.BlockSpec((tm, d), lambda i, k: (i, 0)),
                  _mod_spec(gate_f, m, tm)],
        out_specs=pl.BlockSpec((tm, d), lambda i, k: (i, 0)),
        out_shape=jax.ShapeDtypeStruct((m, d), F32),
        scratch_shapes=[pltpu.VMEM((tm, d), F32)],
        compiler_params=_params("arbitrary", "arbitrary"),
        name="ffn",
    )(h2, w1, w2, x1, gate_f)


def _top_k_indices(gate, k, axis):
    idx = lax.broadcasted_iota(jnp.int32, gate.shape, axis)
    n = gate.shape[axis]
    g = gate
    out = []
    for _ in range(k):
        best = jnp.max(g, axis=axis, keepdims=True)
        first = jnp.min(jnp.where(g == best, idx, n), axis=axis, keepdims=True)
        out.append((first, best))
        g = jnp.where(idx == first, -jnp.inf, g)
    return out


def _top_k_mask(gate, n_valid, k, axis):
    idx = lax.broadcasted_iota(jnp.int32, gate.shape, axis)
    sel = jnp.zeros(gate.shape, jnp.bool_)
    for first, best in _top_k_indices(jnp.where(idx < n_valid, gate, -jnp.inf), k, axis):
        sel = jnp.logical_or(sel, jnp.logical_and(idx == first, best > -jnp.inf))
    return sel


MOBA_GROUP = 4


def _moba_prompt_kernel(qf_ref, qb_ref, kf_ref, kb_ref, vf_ref, o_ref, kmean_ref, vt_ref, bias_ref, s_ref):
    qi = pl.program_id(2)
    blk = MOBA_BLOCK
    n_blocks = kmean_ref.shape[0]

    @pl.when(qi == 0)
    def _():
        for n in range(n_blocks):
            rows = slice(n * blk, (n + 1) * blk)
            kmean_ref[n:n + 1, :] = jnp.sum(kf_ref[rows, :], axis=0, keepdims=True) * (1.0 / blk)
            vt_ref[n] = vf_ref[rows, :].T.astype(BF16)

    gate = lax.dot_general(kmean_ref[...], qf_ref[...], NT_DIMS, precision=lax.Precision.HIGHEST,
                           preferred_element_type=F32)
    bias_ref[...] = jnp.where(_top_k_mask(gate, qi, MOBA_TOPK, 0), 0.0, NEG_BIG)

    q = qb_ref[...]

    def scores(n):
        start = pl.multiple_of(n * blk, blk)
        return lax.dot_general(kb_ref[pl.ds(start, blk), :], q, NT_DIMS, preferred_element_type=F32)

    s = scores(qi)
    key_id = lax.broadcasted_iota(jnp.int32, s.shape, 0)
    qry_id = lax.broadcasted_iota(jnp.int32, s.shape, 1)
    s = jnp.where(key_id <= qry_id, s, NEG_BIG)
    m0 = jnp.max(s, axis=0, keepdims=True)
    p = jnp.exp2(s - m0)
    l0 = jnp.sum(p, axis=0, keepdims=True)
    acc0 = jnp.dot(vt_ref[qi], p.astype(BF16), preferred_element_type=F32)

    last_group = n_blocks // MOBA_GROUP - 1

    def produce(g, slot):
        mx = None
        for i in range(MOBA_GROUP):
            n = g * MOBA_GROUP + i
            s_n = scores(n) + bias_ref[pl.ds(n, 1), :]
            s_ref[slot, i] = s_n
            mx_n = jnp.max(s_n, axis=0, keepdims=True)
            mx = mx_n if mx is None else jnp.maximum(mx, mx_n)
        return mx

    def past_group(g, carry):
        m, l, acc, mx = carry
        slot = g % 2
        m_new = jnp.maximum(m, mx)
        alpha = jnp.exp2(m - m_new)
        l = alpha * l
        acc = alpha * acc
        for i in range(MOBA_GROUP):
            p = jnp.exp2(s_ref[slot, i] - m_new)
            l = l + jnp.sum(p, axis=0, keepdims=True)
            acc = acc + jnp.dot(vt_ref[g * MOBA_GROUP + i], p.astype(BF16), preferred_element_type=F32)
        mx_next = produce(jnp.minimum(g + 1, last_group), 1 - slot)
        return m_new, l, acc, mx_next

    n_groups = lax.div(qi + (MOBA_GROUP - 1), MOBA_GROUP)
    _, l, acc, _ = lax.fori_loop(0, n_groups, past_group, (m0, l0, acc0, produce(0, 0)))
    o_ref[...] = (acc / l).T.astype(o_ref.dtype)


def _moba_prompt_call(qf, qb, kf, kb, vf, batch):
    m, _ = qf.shape
    s = m // batch
    tq = MOBA_BLOCK
    nq = s // tq
    assert s % MOBA_BLOCK == 0 and nq % MOBA_GROUP == 0
    q_spec = pl.BlockSpec((tq, HEAD_DIM), lambda b, h, i: (b * nq + i, h))
    kv_spec = pl.BlockSpec((s, HEAD_DIM), lambda b, h, i: (b, h))
    return pl.pallas_call(
        _moba_prompt_kernel,
        grid=(batch, N_HEADS, nq),
        in_specs=[q_spec, q_spec, kv_spec, kv_spec, kv_spec],
        out_specs=q_spec,
        out_shape=jax.ShapeDtypeStruct((m, WIDTH), BF16),
        scratch_shapes=[pltpu.VMEM((nq, HEAD_DIM), F32), pltpu.VMEM((nq, HEAD_DIM, MOBA_BLOCK), BF16),
                        pltpu.VMEM((nq, tq), F32), pltpu.VMEM((2, MOBA_GROUP, MOBA_BLOCK, tq), F32)],
        compiler_params=_params("arbitrary", "arbitrary", "arbitrary"),
        name="moba_prompt",
    )(qf, qb, kf, kb, vf)


def _neg_softplus2(z2):
    return -(jnp.maximum(z2, 0.0) + jnp.log2(1.0 + jnp.exp2(-jnp.abs(z2))))


def _split_bf16(x):
    hi = x.astype(BF16)
    return hi, (x - hi.astype(F32)).astype(BF16)


def _sb_weights(z2, carry, valid, key_axis):
    tk = z2.shape[key_axis]
    nq = z2.shape[1 - key_axis]
    log_1m = _neg_softplus2(z2)
    if valid is not None:
        log_1m = jnp.where(valid, log_1m, 0.0)
    row = lax.broadcasted_iota(jnp.int32, (tk, tk), 0)
    col = lax.broadcasted_iota(jnp.int32, (tk, tk), 1)
    both = jnp.concatenate(_split_bf16(log_1m), axis=1 - key_axis)
    if key_axis == 0:
        both = jnp.dot(jnp.where(col > row, 1.0, 0.0).astype(BF16), both, preferred_element_type=F32)
        suffix = both[:, :nq] + both[:, nq:]
    else:
        both = jnp.dot(both, jnp.where(row > col, 1.0, 0.0).astype(BF16), preferred_element_type=F32)
        suffix = both[:nq] + both[nq:]
    a = jnp.exp2(z2 + log_1m + suffix + carry)
    if valid is not None:
        a = jnp.where(valid, a, 0.0)
    return a, carry + jnp.sum(log_1m, axis=key_axis, keepdims=True)


SB_TILE = 128
SB_CHAINS = 8


def _sb_prompt_kernel(q_ref, k_ref, vf_ref, o_ref, vt_ref):
    qi = pl.program_id(2)
    t = SB_TILE
    n_tiles = vt_ref.shape[0]

    @pl.when(qi == 0)
    def _():
        for n in range(n_tiles):
            vt_ref[n] = vf_ref[n * t:(n + 1) * t, :].T.astype(BF16)

    width = SB_CHAINS * t
    diag0 = qi * SB_CHAINS
    chain_of_lane = lax.broadcasted_iota(jnp.int32, (1, width), 1) // t

    def walk(j, carry, valid):
        key_tiles = [jnp.maximum(diag0 + r - j, 0) for r in range(SB_CHAINS)]
        z2 = jnp.concatenate(
            [lax.dot_general(k_ref[pl.ds(pl.multiple_of(key_tiles[r] * t, t), t), :], q_ref[r * t:(r + 1) * t, :],
                             NT_DIMS, preferred_element_type=F32) for r in range(SB_CHAINS)], axis=1)
        a, new_carry = _sb_weights(z2, carry, valid, 0)
        has_keys = chain_of_lane + diag0 >= j
        a = jnp.where(has_keys, a, 0.0).astype(BF16)
        out = jnp.concatenate([jnp.dot(vt_ref[key_tiles[r]], a[:, r * t:(r + 1) * t], preferred_element_type=F32)
                               for r in range(SB_CHAINS)], axis=1)
        new_carry = jnp.where(has_keys, new_carry, carry)
        live = jnp.max(jnp.where(chain_of_lane + diag0 >= j + 1, new_carry, SB_STOP))
        return out, new_carry, live

    key_id = lax.broadcasted_iota(jnp.int32, (t, width), 0)
    qry_id = lax.broadcasted_iota(jnp.int32, (t, width), 1) % t
    acc, carry, live = walk(jnp.int32(0), jnp.zeros((1, width), F32), key_id < qry_id)

    def more(state):
        return state[3] > SB_STOP

    def step(state):
        j, acc, carry, _ = state
        out, carry, live = walk(j, carry, None)
        return j + 1, acc + out, carry, live

    _, acc, _, _ = lax.while_loop(more, step, (jnp.int32(1), acc, carry, live))
    o_ref[...] = acc.T.astype(o_ref.dtype)


def _sb_prompt_call(qb, kb, vf, batch):
    m, _ = qb.shape
    s = m // batch
    tq = SB_TILE * SB_CHAINS
    nq = s // tq
    assert s % tq == 0
    q_spec = pl.BlockSpec((tq, HEAD_DIM), lambda b, h, i: (b * nq + i, h))
    kv_spec = pl.BlockSpec((s, HEAD_DIM), lambda b, h, i: (b, h))
    return pl.pallas_call(
        _sb_prompt_kernel,
        grid=(batch, N_HEADS, nq),
        in_specs=[q_spec, kv_spec, kv_spec],
        out_specs=q_spec,
        out_shape=jax.ShapeDtypeStruct((m, WIDTH), BF16),
        scratch_shapes=[pltpu.VMEM((s // SB_TILE, HEAD_DIM, SB_TILE), BF16)],
        compiler_params=_params("arbitrary", "arbitrary", "arbitrary"),
        name="sb_prompt",
    )(qb, kb, vf)


def _sb_sample_kernel(pt_ref, q_ref, kn_ref, vn_ref, ck_ref, cv_ref, o_ref, kbuf, vbuf, sem, *, n_tok, n_pages):
    b = pl.program_id(0)
    rows = N_HEADS * SUBLANES

    def page_copies(p, slot):
        phys = pt_ref[b * n_pages + p]
        return (pltpu.make_async_copy(ck_ref.at[phys], kbuf.at[slot], sem.at[slot, 0]),
                pltpu.make_async_copy(cv_ref.at[phys], vbuf.at[slot], sem.at[slot, 1]))

    def slot_of(p):
        return (n_pages - 1 - p) % 2

    for c in page_copies(n_pages - 1, 0):
        c.start()

    def head_q(h):
        return q_ref[0, :, h * HEAD_DIM:(h + 1) * HEAD_DIM].astype(BF16)

    def tile(k_of_head, v_of_head, carry, valid):
        z2 = jnp.concatenate([lax.dot_general(head_q(h), k_of_head(h), NT_DIMS, preferred_element_type=F32)
                              for h in range(N_HEADS)], axis=0)
        a, carry = _sb_weights(z2, carry, valid, 1)
        a = a.astype(BF16)
        out = jnp.concatenate([jnp.dot(a[h * SUBLANES:(h + 1) * SUBLANES], v_of_head(h), preferred_element_type=F32)
                               for h in range(N_HEADS)], axis=0)
        return out, carry

    tk = kn_ref.shape[1]
    key = lax.broadcasted_iota(jnp.int32, (rows, tk), 1)
    tok = lax.broadcasted_iota(jnp.int32, (rows, tk), 0) % SUBLANES
    acc, carry = tile(lambda h: kn_ref[0, :, h * HEAD_DIM:(h + 1) * HEAD_DIM],
                      lambda h: vn_ref[0, :, h * HEAD_DIM:(h + 1) * HEAD_DIM],
                      jnp.zeros((rows, 1), F32), key < tok)
    tok_of_row = lax.broadcasted_iota(jnp.int32, (rows, 1), 0) % SUBLANES

    def more(state):
        p, _, carry = state
        live = jnp.max(jnp.where(tok_of_row < n_tok, carry, SB_STOP)) > SB_STOP
        return jnp.logical_and(p >= 0, live)

    def step(state):
        p, acc, carry = state
        slot = slot_of(p)
        for c in page_copies(p, slot):
            c.wait()

        @pl.when(p > 0)
        def _():
            for c in page_copies(p - 1, 1 - slot):
                c.start()

        out, carry = tile(lambda h: kbuf[slot, :, h, :].astype(BF16), lambda h: vbuf[slot, :, h, :].astype(BF16),
                          carry, None)
        return p - 1, acc + out, carry

    p, acc, _ = lax.while_loop(more, step, (jnp.int32(n_pages - 1), acc, carry))

    @pl.when(p >= 0)
    def _():
        for c in page_copies(p, slot_of(p)):
            c.wait()

    for h in range(N_HEADS):
        o_ref[0, :, h * HEAD_DIM:(h + 1) * HEAD_DIM] = acc[h * SUBLANES:(h + 1) * SUBLANES, :].astype(o_ref.dtype)


def _sb_sample_call(q_pad, k_new, v_new, cache_k, cache_v, page_table, n_tok):
    db, rows, w = q_pad.shape
    n_pages = page_table.shape[1]
    per_b = lambda b, pt: (b, 0, 0)
    any_spec = pl.BlockSpec(memory_space=pl.ANY)
    page_buf = pltpu.VMEM((2, PAGE_SIZE, N_HEADS, HEAD_DIM), F32)
    return pl.pallas_call(
        functools.partial(_sb_sample_kernel, n_tok=n_tok, n_pages=n_pages),
        grid_spec=pltpu.PrefetchScalarGridSpec(
            num_scalar_prefetch=1,
            grid=(db,),
            in_specs=[pl.BlockSpec((1, rows, w), per_b),
                      pl.BlockSpec((1, PAGE_SIZE, w), per_b), pl.BlockSpec((1, PAGE_SIZE, w), per_b),
                      any_spec, any_spec],
            out_specs=pl.BlockSpec((1, rows, w), per_b),
            scratch_shapes=[page_buf, page_buf, pltpu.SemaphoreType.DMA((2, 2))]),
        out_shape=jax.ShapeDtypeStruct((db, rows, w), BF16),
        compiler_params=_params("arbitrary"),
        name="sb_sample",
    )(page_table.reshape(-1), q_pad, k_new, v_new, cache_k, cache_v)


PAGES_PER_STEP = 16


def _page_sum_kernel(pt_ref, *refs):
    del pt_ref
    o_ref = refs[-1]
    for r in range(o_ref.shape[1]):
        tot = jnp.sum(refs[r * PAGES_PER_BLOCK][0], axis=0)
        for j in range(1, PAGES_PER_BLOCK):
            tot = tot + jnp.sum(refs[r * PAGES_PER_BLOCK + j][0], axis=0)
        o_ref[0, r] = tot


def _block_sums_call(cache_k, page_table):
    db, n_pages = page_table.shape
    pps = min(PAGES_PER_STEP, n_pages)
    assert n_pages % pps == 0

    def page(j):
        return lambda b, g, pt: (pt[b * n_pages + g * pps + j], 0, 0, 0)

    return pl.pallas_call(
        _page_sum_kernel,
        grid_spec=pltpu.PrefetchScalarGridSpec(
            num_scalar_prefetch=1,
            grid=(db, n_pages // pps),
            in_specs=[pl.BlockSpec((1, PAGE_SIZE, N_HEADS, HEAD_DIM), page(j)) for j in range(pps)],
            out_specs=pl.BlockSpec((1, pps // PAGES_PER_BLOCK, N_HEADS, HEAD_DIM), lambda b, g, pt: (b, g, 0, 0))),
        out_shape=jax.ShapeDtypeStruct((db, n_pages // PAGES_PER_BLOCK, N_HEADS, HEAD_DIM), F32),
        compiler_params=_params("arbitrary", "arbitrary"),
        name="moba_block_sums",
    )(page_table.reshape(-1), *([cache_k] * pps))


def _moba_select_kernel(bsum_ref, q_ref, o_ref):
    q = q_ref[0]
    gates = [lax.dot_general(q[:, h * HEAD_DIM:(h + 1) * HEAD_DIM], bsum_ref[0, :, h, :] * (1.0 / MOBA_BLOCK),
                             NT_DIMS, precision=lax.Precision.HIGHEST, preferred_element_type=F32)
             for h in range(N_HEADS)]
    gate = jnp.concatenate(gates, axis=0)
    lane = lax.broadcasted_iota(jnp.int32, o_ref.shape[1:], 1)
    out = jnp.zeros(o_ref.shape[1:], jnp.int32)
    for r, (idx, _) in enumerate(_top_k_indices(gate, MOBA_TOPK, 1)):
        out = jnp.where(lane == r, idx, out)
    o_ref[0] = out


def _moba_select_call(block_sums, q_pad):
    db, n_blocks = block_sums.shape[:2]
    rows, w = q_pad.shape[1:]
    return pl.pallas_call(
        _moba_select_kernel,
        grid=(db,),
        in_specs=[pl.BlockSpec((1, n_blocks, N_HEADS, HEAD_DIM), lambda b: (b, 0, 0, 0)),
                  pl.BlockSpec((1, rows, w), lambda b: (b, 0, 0))],
        out_specs=pl.BlockSpec((1, N_HEADS * rows, LANES), lambda b: (b, 0, 0)),
        out_shape=jax.ShapeDtypeStruct((db, N_HEADS * rows, LANES), jnp.int32),
        compiler_params=_params("arbitrary"),
        name="moba_select",
    )(block_sums, q_pad)


def _moba_sample_kernel(sel_ref, pt_ref, q_ref, kn_ref, vn_ref, ck_ref, cv_ref, o_ref, kbuf, vbuf, sem,
                        *, n_tok, n_pages):
    step = pl.program_id(0)
    n_steps = pl.num_programs(0)
    per_tok = MOBA_TOPK * PAGES_PER_BLOCK

    def copies(s, slot):
        b = s // N_HEADS
        h = s % N_HEADS
        out = []
        for t in range(n_tok):
            for r in range(MOBA_TOPK):
                blk = sel_ref[(s * n_tok + t) * MOBA_TOPK + r]
                for j in range(PAGES_PER_BLOCK):
                    phys = pt_ref[b * n_pages + blk * PAGES_PER_BLOCK + j]
                    i = (t * MOBA_TOPK + r) * PAGES_PER_BLOCK + j
                    out.append(pltpu.make_async_copy(ck_ref.at[phys, :, h, :], kbuf.at[slot, i], sem.at[slot, 0]))
                    out.append(pltpu.make_async_copy(cv_ref.at[phys, :, h, :], vbuf.at[slot, i], sem.at[slot, 1]))
        return out

    slot = step % 2

    @pl.when(step == 0)
    def _():
        for c in copies(step, slot):
            c.start()

    @pl.when(step + 1 < n_steps)
    def _():
        for c in copies(step + 1, 1 - slot):
            c.start()

    for c in copies(step, slot):
        c.wait()

    q = q_ref[0].astype(BF16)
    row = lax.broadcasted_iota(jnp.int32, (SUBLANES, 1), 0)
    s_sel = jnp.zeros((SUBLANES, per_tok * PAGE_SIZE), F32)
    for t in range(n_tok):
        k_t = kbuf[slot, t * per_tok:(t + 1) * per_tok].reshape(per_tok * PAGE_SIZE, HEAD_DIM).astype(BF16)
        s_sel = jnp.where(row == t, lax.dot_general(q, k_t, NT_DIMS, preferred_element_type=F32), s_sel)
    s_own = lax.dot_general(q, kn_ref[0].astype(BF16), NT_DIMS, preferred_element_type=F32)
    key = lax.broadcasted_iota(jnp.int32, s_own.shape, 1)
    s_own = jnp.where(key <= row, s_own, NEG_BIG)
    m = jnp.maximum(jnp.max(s_sel, axis=-1, keepdims=True), jnp.max(s_own, axis=-1, keepdims=True))
    p_sel = jnp.exp(s_sel - m)
    p_own = jnp.exp(s_own - m)
    denom = jnp.sum(p_sel, axis=-1, keepdims=True) + jnp.sum(p_own, axis=-1, keepdims=True)
    out = jnp.dot(p_own.astype(BF16), vn_ref[0].astype(BF16), preferred_element_type=F32)
    p_sel = p_sel.astype(BF16)
    for t in range(n_tok):
        v_t = vbuf[slot, t * per_tok:(t + 1) * per_tok].reshape(per_tok * PAGE_SIZE, HEAD_DIM).astype(BF16)
        out = out + jnp.where(row == t, jnp.dot(p_sel, v_t, preferred_element_type=F32), 0.0)
    o_ref[0] = (out / denom).astype(o_ref.dtype)


def _moba_sample_call(sel, page_table, q_pad, k_pad, v_pad, cache_k, cache_v, n_tok):
    db, n_pages = page_table.shape
    rows, w = q_pad.shape[1:]
    n_sel = n_tok * MOBA_TOPK * PAGES_PER_BLOCK
    new_spec = pl.BlockSpec((1, rows, HEAD_DIM), lambda s, *_: (s // N_HEADS, 0, s % N_HEADS))
    any_spec = pl.BlockSpec(memory_space=pl.ANY)
    return pl.pallas_call(
        functools.partial(_moba_sample_kernel, n_tok=n_tok, n_pages=n_pages),
        grid_spec=pltpu.PrefetchScalarGridSpec(
            num_scalar_prefetch=2,
            grid=(db * N_HEADS,),
            in_specs=[new_spec, new_spec, new_spec, any_spec, any_spec],
            out_specs=new_spec,
            scratch_shapes=[pltpu.VMEM((2, n_sel, PAGE_SIZE, HEAD_DIM), F32),
                            pltpu.VMEM((2, n_sel, PAGE_SIZE, HEAD_DIM), F32),
                            pltpu.SemaphoreType.DMA((2, 2))]),
        out_shape=jax.ShapeDtypeStruct((db, rows, w), BF16),
        compiler_params=_params("arbitrary"),
        name="moba_sample",
    )(sel.reshape(-1), page_table.reshape(-1), q_pad, k_pad, v_pad, cache_k, cache_v)


def _rope_tables(pos):
    half = HEAD_DIM // 2
    inv = ROPE_THETA ** (-jnp.arange(half, dtype=F32) / half)
    ang = pos.astype(F32)[:, None] * inv[None, :]
    cos, sin = jnp.cos(ang), jnp.sin(ang)
    return jnp.concatenate([cos, cos], axis=-1), jnp.concatenate([-sin, sin], axis=-1)


def _split_cols(w, bounds):
    return [w[:, a:b] for a, b in zip(bounds[:-1], bounds[1:])]


def _attn_inputs(x, mods, gain, w_parts, q_gain, k_gain, cos, sin, tm):
    shift_a, scale_a = mods
    h = _norm_call(x, gain, scale_a, shift_a, tm)
    w_qm, w_km, w_vm, w_qs, w_ks, w_vs, w_gm, w_gs = w_parts
    q_scale = ATTN_SCALE * LOG2E
    qm_f, qm_b = _proj_call("rope", h, w_qm, tm, out_scale=q_scale, gain=q_gain, cos=cos, sin=sin)
    km_f, km_b = _proj_call("rope", h, w_km, tm, gain=k_gain, cos=cos, sin=sin)
    vm_f, = _proj_call("raw", h, w_vm, tm, dtypes=(F32,))
    qs_b, = _proj_call("raw", h, w_qs, tm, out_scale=q_scale, dtypes=(BF16,))
    ks_f, ks_b = _proj_call("raw", h, w_ks, tm)
    vs_f, = _proj_call("raw", h, w_vs, tm, dtypes=(F32,))
    g_m = _proj_call("sigmoid", h, w_gm, tm)
    g_s = _proj_call("sigmoid", h, w_gs, tm)
    return dict(qm_f=qm_f, qm_b=qm_b, km_f=km_f, km_b=km_b, vm_f=vm_f,
                qs_b=qs_b, ks_f=ks_f, ks_b=ks_b, vs_f=vs_f, g_m=g_m, g_s=g_s)


def kernel(x_prompt, x_sample, cache_k_moba, cache_v_moba, cache_k_sb, cache_v_sb, page_table, c_prompt, c_sample, w_ada, b_ada, attn_norm_g, w_in, q_norm_g, k_norm_g, w_br_moba, w_br_sb, w_out, mlp_norm_g, w_ff1, w_ff2):
    batch, seq, d = x_prompt.shape
    db, n_tok, _ = x_sample.shape
    depth, n_phys = cache_k_moba.shape[:2]
    n_pages = page_table.shape[1]
    past_len = n_pages * PAGE_SIZE
    assert past_len % MOBA_BLOCK == 0 and past_len // MOBA_BLOCK >= MOBA_TOPK and n_tok <= SUBLANES
    m_p, m_s = batch * seq, db * n_tok
    tm_p, tm_s = 512, m_s
    assert m_s % SUBLANES == 0 and seq % tm_p == 0

    cos_p, sin_p = _rope_tables(jnp.arange(seq, dtype=jnp.int32))
    cos_s, sin_s = _rope_tables(past_len + jnp.arange(n_tok, dtype=jnp.int32))
    cos_s, sin_s = jnp.tile(cos_s, (db, 1)), jnp.tile(sin_s, (db, 1))

    n_c = batch + db
    c_rows = -(-n_c // SUBLANES) * SUBLANES
    c_all = jnp.pad(jnp.concatenate([c_prompt, c_sample], axis=0), ((0, c_rows - n_c), (0, 0)))
    bounds = [0, WIDTH, 2 * WIDTH, 3 * WIDTH, 4 * WIDTH, 5 * WIDTH, 6 * WIDTH, 6 * WIDTH + d, 6 * WIDTH + 2 * d]
    cache_shape = (depth * n_phys, PAGE_SIZE, N_HEADS, HEAD_DIM)
    ck_m, cv_m = cache_k_moba.reshape(cache_shape), cache_v_moba.reshape(cache_shape)
    ck_s, cv_s = cache_k_sb.reshape(cache_shape), cache_v_sb.reshape(cache_shape)

    def pad_rows(v, rows):
        return jnp.pad(v.reshape(db, n_tok, WIDTH), ((0, 0), (0, rows - n_tok), (0, 0)))

    xp = x_prompt.reshape(m_p, d)
    xs = x_sample.reshape(m_s, d)
    outs = [[] for _ in range(8)]
    for l in range(depth):
        mods = _ada_mods(c_all, w_ada[l], b_ada[l])
        mods_p = [mods[:batch, i * d:(i + 1) * d].reshape(batch, 1, d) for i in range(N_MOD)]
        mods_s = [jnp.repeat(mods[batch:n_c, i * d:(i + 1) * d], n_tok, axis=0).reshape(1, m_s, d)
                  for i in range(N_MOD)]
        w_parts = _split_cols(w_in[l].astype(BF16), bounds)
        w_m, w_s, w_o = w_br_moba[l].astype(BF16), w_br_sb[l].astype(BF16), w_out[l].astype(BF16)
        w_1, w_2 = w_ff1[l].astype(BF16), w_ff2[l].astype(BF16)

        a = _attn_inputs(xp, mods_p[:2], attn_norm_g[l], w_parts, q_norm_g[l], k_norm_g[l], cos_p, sin_p, tm_p)
        o_m = _moba_prompt_call(a["qm_f"], a["qm_b"], a["km_f"], a["km_b"], a["vm_f"], batch)
        o_s = _sb_prompt_call(a["qs_b"], a["ks_b"], a["vs_f"], batch)
        x1, h2 = _merge_call(o_m, o_s, a["g_m"], a["g_s"], xp, w_m, w_s, w_o, mods_p[2], mlp_norm_g[l],
                             mods_p[4], mods_p[3], 256)
        xp = _ffn_call(h2, w_1, w_2, x1, mods_p[5], tm_p, 1024)
        for dst, key in zip(outs[:4], ("km_f", "vm_f", "ks_f", "vs_f")):
            dst.append(a[key].reshape(batch, seq, N_HEADS, HEAD_DIM))

        a = _attn_inputs(xs, mods_s[:2], attn_norm_g[l], w_parts, q_norm_g[l], k_norm_g[l], cos_s, sin_s, tm_s)
        pt = page_table + l * n_phys
        qm_pad = pad_rows(a["qm_f"], SUBLANES)
        block_sums = _block_sums_call(ck_m, pt)
        picks = _moba_select_call(block_sums, qm_pad)
        sel = picks.reshape(db, N_HEADS, SUBLANES, LANES)[:, :, :n_tok, :MOBA_TOPK]
        o_m = _moba_sample_call(sel, pt, qm_pad * ATTN_SCALE, pad_rows(a["km_f"], SUBLANES),
                                pad_rows(a["vm_f"], SUBLANES), ck_m, cv_m, n_tok)
        o_m = o_m[:, :n_tok].reshape(m_s, WIDTH)
        o_s = _sb_sample_call(pad_rows(a["qs_b"], SUBLANES).astype(F32), pad_rows(a["ks_b"], PAGE_SIZE),
                              pad_rows(a["vs_f"], PAGE_SIZE).astype(BF16), ck_s, cv_s, pt, n_tok)
        o_s = o_s[:, :n_tok].reshape(m_s, WIDTH)
        x1, h2 = _merge_call(o_m, o_s, a["g_m"], a["g_s"], xs, w_m, w_s, w_o, mods_s[2], mlp_norm_g[l],
                             mods_s[4], mods_s[3], tm_s)
        xs = _ffn_call(h2, w_1, w_2, x1, mods_s[5], tm_s, 1024)
        for dst, key in zip(outs[4:], ("km_f", "vm_f", "ks_f", "vs_f")):
            dst.append(a[key].reshape(db, n_tok, N_HEADS, HEAD_DIM))

    return (xp.reshape(batch, seq, d), xs.reshape(db, n_tok, d), *[jnp.stack(o) for o in outs])
```

```python
import functools

import jax
import jax.numpy as jnp
from jax import lax
from jax.experimental import pallas as pl
from jax.experimental.pallas import tpu as pltpu

HEAD_DIM = 128
N_HEADS = 8
WIDTH = N_HEADS * HEAD_DIM
MOBA_BLOCK = 256
MOBA_TOPK = 3
PAGE_SIZE = 128
PAGES_PER_BLOCK = MOBA_BLOCK // PAGE_SIZE
ROPE_THETA = 10000.0
NORM_EPS = 1e-6
N_MOD = 6
ATTN_SCALE = HEAD_DIM ** -0.5
LOG2E = 1.4426950408889634

LANES = 128
SUBLANES = 8
VMEM_LIMIT_BYTES = 56 * 1024 * 1024
NEG_BIG = -1e30
SB_STOP = -120.0 * LOG2E

F32 = jnp.float32
BF16 = jnp.bfloat16
NT_DIMS = (((1,), (1,)), ((), ()))


def _params(*sem):
    return pltpu.CompilerParams(dimension_semantics=sem, vmem_limit_bytes=VMEM_LIMIT_BYTES)


def _mods_kernel(c_ref, w_ref, b_ref, o_ref):
    o_ref[...] = jnp.dot(c_ref[...], w_ref[...], precision=lax.Precision.HIGHEST,
                         preferred_element_type=F32) + b_ref[...]


def _ada_mods(c, w, b):
    r, d = c.shape
    n = w.shape[1]
    tn = 512
    return pl.pallas_call(
        _mods_kernel,
        grid=(n // tn,),
        in_specs=[pl.BlockSpec((r, d), lambda j: (0, 0)),
                  pl.BlockSpec((d, tn), lambda j: (0, j)),
                  pl.BlockSpec((1, tn), lambda j: (0, j))],
        out_specs=pl.BlockSpec((r, tn), lambda j: (0, j)),
        out_shape=jax.ShapeDtypeStruct((r, n), F32),
        compiler_params=_params("arbitrary"),
        name="ada_mods",
    )(c, w, b.reshape(1, n))


def _norm_mod(x, gain, scale, shift):
    ms = jnp.mean(x * x, axis=-1, keepdims=True)
    return (x * lax.rsqrt(ms + NORM_EPS) * gain) * (1.0 + scale) + shift


def _norm_mod_kernel(x_ref, g_ref, sc_ref, sh_ref, o_ref):
    o_ref[...] = _norm_mod(x_ref[...], g_ref[...], sc_ref[0], sh_ref[0]).astype(o_ref.dtype)


def _mod_spec(mod, m, tm):
    g, r, d = mod.shape
    tiles_per_group = (m // g) // tm
    return pl.BlockSpec((1, r, d), lambda i, *_: (i // tiles_per_group, 0, 0))


def _norm_call(x, gain, scale, shift, tm):
    m, d = x.shape
    return pl.pallas_call(
        _norm_mod_kernel,
        grid=(m // tm,),
        in_specs=[pl.BlockSpec((tm, d), lambda i: (i, 0)),
                  pl.BlockSpec((1, d), lambda i: (0, 0)),
                  _mod_spec(scale, m, tm), _mod_spec(shift, m, tm)],
        out_specs=pl.BlockSpec((tm, d), lambda i: (i, 0)),
        out_shape=jax.ShapeDtypeStruct((m, d), BF16),
        compiler_params=_params("arbitrary"),
        name="norm_mod",
    )(x, gain.reshape(1, d), scale, shift)


def _proj_rope_kernel(h_ref, w_ref, g_ref, cos_ref, sin_ref, of_ref, ob_ref, *, out_scale):
    z = jnp.dot(h_ref[...], w_ref[...], preferred_element_type=F32)
    g = g_ref[...]
    cos = cos_ref[...]
    sin = sin_ref[...]
    for hd in range(z.shape[1] // HEAD_DIM):
        sl = slice(hd * HEAD_DIM, (hd + 1) * HEAD_DIM)
        zh = z[:, sl]
        y = zh * lax.rsqrt(jnp.mean(zh * zh, axis=-1, keepdims=True) + NORM_EPS) * g
        y = y * cos + pltpu.roll(y, HEAD_DIM // 2, 1) * sin
        of_ref[:, sl] = y
        ob_ref[:, sl] = (y * out_scale).astype(ob_ref.dtype)


def _proj_raw_kernel(h_ref, w_ref, *o_refs, out_scale):
    z = jnp.dot(h_ref[...], w_ref[...], preferred_element_type=F32)
    for o_ref in o_refs:
        o_ref[...] = z if o_ref.dtype == F32 else (z * out_scale).astype(o_ref.dtype)


def _proj_sigmoid_kernel(h_ref, w_ref, o_ref):
    z = jnp.dot(h_ref[...], w_ref[...], preferred_element_type=F32)
    o_ref[...] = 1.0 / (1.0 + jnp.exp(-z))


def _proj_call(kind, h, w, tm, *, out_scale=1.0, gain=None, cos=None, sin=None, dtypes=(F32, BF16)):
    m, d = h.shape
    n = w.shape[1]
    row = lambda i: (i, 0)
    h_spec = pl.BlockSpec((tm, d), row)
    w_spec = pl.BlockSpec((d, n), lambda i: (0, 0))
    o_spec = pl.BlockSpec((tm, n), row)
    two_out = dict(out_specs=[o_spec, o_spec],
                   out_shape=[jax.ShapeDtypeStruct((m, n), F32), jax.ShapeDtypeStruct((m, n), BF16)])
    if kind == "rope":
        tab_tiles = cos.shape[0] // tm
        tab_spec = pl.BlockSpec((tm, HEAD_DIM), lambda i: (i % tab_tiles, 0))
        return pl.pallas_call(
            functools.partial(_proj_rope_kernel, out_scale=out_scale),
            grid=(m // tm,),
            in_specs=[h_spec, w_spec, pl.BlockSpec((1, HEAD_DIM), lambda i: (0, 0)), tab_spec, tab_spec],
            compiler_params=_params("arbitrary"), name="proj_rope", **two_out,
        )(h, w, gain.reshape(1, HEAD_DIM), cos, sin)
    if kind == "raw":
        return pl.pallas_call(
            functools.partial(_proj_raw_kernel, out_scale=out_scale),
            grid=(m // tm,), in_specs=[h_spec, w_spec],
            out_specs=[o_spec] * len(dtypes),
            out_shape=[jax.ShapeDtypeStruct((m, n), dt) for dt in dtypes],
            compiler_params=_params("arbitrary"), name="proj_raw",
        )(h, w)
    assert kind == "sigmoid"
    return pl.pallas_call(
        _proj_sigmoid_kernel,
        grid=(m // tm,), in_specs=[h_spec, w_spec], out_specs=o_spec,
        out_shape=jax.ShapeDtypeStruct((m, n), F32),
        compiler_params=_params("arbitrary"), name="proj_sigmoid",
    )(h, w)


def _merge_kernel(om_ref, os_ref, gm_ref, gs_ref, x_ref, wm_ref, ws_ref, wo_ref,
                  ga_ref, g2_ref, sc_ref, sh_ref, x1_ref, h2_ref):
    u = (gm_ref[...] * jnp.dot(om_ref[...], wm_ref[...], preferred_element_type=F32)
         + gs_ref[...] * jnp.dot(os_ref[...], ws_ref[...], preferred_element_type=F32))
    x1 = x_ref[...] + ga_ref[0] * jnp.dot(u.astype(BF16), wo_ref[...], preferred_element_type=F32)
    x1_ref[...] = x1
    h2_ref[...] = _norm_mod(x1, g2_ref[...], sc_ref[0], sh_ref[0]).astype(h2_ref.dtype)


def _merge_call(o_m, o_s, g_m, g_s, x, w_m, w_s, w_o, gate_a, gain2, scale_f, shift_f, tm):
    m, d = x.shape
    wd = o_m.shape[1]
    row = lambda i: (i, 0)
    const = lambda i: (0, 0)
    resident = functools.partial(pl.BlockSpec, index_map=const, pipeline_mode=pl.Buffered(1))
    return pl.pallas_call(
        _merge_kernel,
        grid=(m // tm,),
        in_specs=[pl.BlockSpec((tm, wd), row), pl.BlockSpec((tm, wd), row),
                  pl.BlockSpec((tm, d), row), pl.BlockSpec((tm, d), row), pl.BlockSpec((tm, d), row),
                  resident((wd, d)), resident((wd, d)), resident((d, d)),
                  _mod_spec(gate_a, m, tm), pl.BlockSpec((1, d), const),
                  _mod_spec(scale_f, m, tm), _mod_spec(shift_f, m, tm)],
        out_specs=[pl.BlockSpec((tm, d), row), pl.BlockSpec((tm, d), row)],
        out_shape=[jax.ShapeDtypeStruct((m, d), F32), jax.ShapeDtypeStruct((m, d), BF16)],
        compiler_params=_params("arbitrary"),
        name="merge",
    )(o_m, o_s, g_m, g_s, x, w_m, w_s, w_o, gate_a, gain2.reshape(1, d), scale_f, shift_f)


def _ffn_kernel(h_ref, w1_ref, w2_ref, x_ref, gf_ref, o_ref, acc_ref):
    k = pl.program_id(1)

    @pl.when(k == 0)
    def _():
        acc_ref[...] = jnp.zeros_like(acc_ref)

    a = jnp.dot(h_ref[...], w1_ref[...], preferred_element_type=F32)
    a = jnp.square(jnp.maximum(a, 0.0)).astype(BF16)
    acc_ref[...] += jnp.dot(a, w2_ref[...], preferred_element_type=F32)

    @pl.when(k == pl.num_programs(1) - 1)
    def _():
        o_ref[...] = x_ref[...] + gf_ref[0] * acc_ref[...]


def _ffn_call(h2, w1, w2, x1, gate_f, tm, tf):
    m, d = x1.shape
    f = w1.shape[1]
    return pl.pallas_call(
        _ffn_kernel,
        grid=(m // tm, f // tf),
        in_specs=[pl.BlockSpec((tm, d), lambda i, k: (i, 0)),
                  pl.BlockSpec((d, tf), lambda i, k: (0, k)),
                  pl.BlockSpec((tf, d), lambda i, k: (k, 0)),
                  pl.BlockSpec((tm, d), lambda i, k: (i, 0)),
                  _mod_spec(gate_f, m, tm)],
        out_specs=pl.BlockSpec((tm, d), lambda i, k: (i, 0)),
        out_shape=jax.ShapeDtypeStruct((m, d), F32),
        scratch_shapes=[pltpu.VMEM((tm, d), F32)],
        compiler_params=_params("arbitrary", "arbitrary"),
        name="ffn",
    )(h2, w1, w2, x1, gate_f)


def _top_k_indices(gate, k, axis):
    idx = lax.broadcasted_iota(jnp.int32, gate.shape, axis)
    n = gate.shape[axis]
    g = gate
    out = []
    for _ in range(k):
        best = jnp.max(g, axis=axis, keepdims=True)
        first = jnp.min(jnp.where(g == best, idx, n), axis=axis, keepdims=True)
        out.append((first, best))
        g = jnp.where(idx == first, -jnp.inf, g)
    return out


def _top_k_mask(gate, n_valid, k, axis):
    idx = lax.broadcasted_iota(jnp.int32, gate.shape, axis)
    sel = jnp.zeros(gate.shape, jnp.bool_)
    for first, best in _top_k_indices(jnp.where(idx < n_valid, gate, -jnp.inf), k, axis):
        sel = jnp.logical_or(sel, jnp.logical_and(idx == first, best > -jnp.inf))
    return sel


MOBA_GROUP = 4


def _moba_prompt_kernel(pt_ref, qf_ref, qb_ref, kf_ref, kb_ref, vf_ref, ck_ref, o_ref, bsum_ref,
                        kmean_ref, vt_ref, bias_ref, s_ref, pbuf, psem, *, units_per_step, n_units):
    qi = pl.program_id(2)
    blk = MOBA_BLOCK
    n_blocks = kmean_ref.shape[0]
    step = (pl.program_id(0) * pl.num_programs(1) + pl.program_id(1)) * pl.num_programs(2) + qi
    stream_steps = n_units // units_per_step
    pages_per_step = units_per_step * PAGES_PER_BLOCK
    slot_p = step % 2

    def stream_copies(s, slot):
        return [pltpu.make_async_copy(ck_ref.at[pt_ref[s * pages_per_step + j]], pbuf.at[slot, j], psem.at[slot])
                for j in range(pages_per_step)]

    @pl.when(step == 0)
    def _():
        for c in stream_copies(step, slot_p):
            c.start()

    @pl.when(step + 1 < stream_steps)
    def _():
        for c in stream_copies(step + 1, 1 - slot_p):
            c.start()

    @pl.when(qi == 0)
    def _():
        for n in range(n_blocks):
            rows = slice(n * blk, (n + 1) * blk)
            kmean_ref[n:n + 1, :] = jnp.sum(kf_ref[rows, :], axis=0, keepdims=True) * (1.0 / blk)
            vt_ref[n] = vf_ref[rows, :].T.astype(BF16)

    gate = lax.dot_general(kmean_ref[...], qf_ref[...], NT_DIMS, precision=lax.Precision.HIGHEST,
                           preferred_element_type=F32)
    bias_ref[...] = jnp.where(_top_k_mask(gate, qi, MOBA_TOPK, 0), 0.0, NEG_BIG)

    q = qb_ref[...]

    def scores(n):
        start = pl.multiple_of(n * blk, blk)
        return lax.dot_general(kb_ref[pl.ds(start, blk), :], q, NT_DIMS, preferred_element_type=F32)

    s = scores(qi)
    key_id = lax.broadcasted_iota(jnp.int32, s.shape, 0)
    qry_id = lax.broadcasted_iota(jnp.int32, s.shape, 1)
    s = jnp.where(key_id <= qry_id, s, NEG_BIG)
    m0 = jnp.max(s, axis=0, keepdims=True)
    p = jnp.exp2(s - m0)
    l0 = jnp.sum(p, axis=0, keepdims=True)
    acc0 = jnp.dot(vt_ref[qi], p.astype(BF16), preferred_element_type=F32)

    last_group = n_blocks // MOBA_GROUP - 1

    def produce(g, slot):
        mx = None
        for i in range(MOBA_GROUP):
            n = g * MOBA_GROUP + i
            s_n = scores(n) + bias_ref[pl.ds(n, 1), :]
            s_ref[slot, i] = s_n
            mx_n = jnp.max(s_n, axis=0, keepdims=True)
            mx = mx_n if mx is None else jnp.maximum(mx, mx_n)
        return mx

    def past_group(g, carry):
        m, l, acc, mx = carry
        slot = g % 2
        m_new = jnp.maximum(m, mx)
        alpha = jnp.exp2(m - m_new)
        l = alpha * l
        acc = alpha * acc
        for i in range(MOBA_GROUP):
            p = jnp.exp2(s_ref[slot, i] - m_new)
            l = l + jnp.sum(p, axis=0, keepdims=True)
            acc = acc + jnp.dot(vt_ref[g * MOBA_GROUP + i], p.astype(BF16), preferred_element_type=F32)
        mx_next = produce(jnp.minimum(g + 1, last_group), 1 - slot)
        return m_new, l, acc, mx_next

    n_groups = lax.div(qi + (MOBA_GROUP - 1), MOBA_GROUP)
    _, l, acc, _ = lax.fori_loop(0, n_groups, past_group, (m0, l0, acc0, produce(0, 0)))
    o_ref[...] = (acc / l).T.astype(o_ref.dtype)

    @pl.when(step < stream_steps)
    def _():
        for c in stream_copies(step, slot_p):
            c.wait()
        for u in range(units_per_step):
            tot = jnp.sum(pbuf[slot_p, u * PAGES_PER_BLOCK], axis=0)
            for j in range(1, PAGES_PER_BLOCK):
                tot = tot + jnp.sum(pbuf[slot_p, u * PAGES_PER_BLOCK + j], axis=0)
            bsum_ref[0, u] = tot


def _moba_prompt_call(qf, qb, kf, kb, vf, batch, cache_k, page_table):
    m, _ = qf.shape
    s = m // batch
    tq = MOBA_BLOCK
    nq = s // tq
    assert s % MOBA_BLOCK == 0 and nq % MOBA_GROUP == 0
    db, n_pages = page_table.shape
    blocks_per_sample = n_pages // PAGES_PER_BLOCK
    n_units = db * blocks_per_sample
    n_steps = batch * N_HEADS * nq
    units_per_step = next(u for u in range(1, blocks_per_sample + 1)
                          if blocks_per_sample % u == 0 and u * n_steps >= n_units)
    last_unit = n_units - units_per_step

    def bsum_index(b, h, i, pt):
        unit = jnp.minimum(((b * N_HEADS + h) * nq + i) * units_per_step, last_unit)
        return (unit // blocks_per_sample, (unit % blocks_per_sample) // units_per_step, 0, 0)

    q_spec = pl.BlockSpec((tq, HEAD_DIM), lambda b, h, i, pt: (b * nq + i, h))
    kv_spec = pl.BlockSpec((s, HEAD_DIM), lambda b, h, i, pt: (b, h))
    return pl.pallas_call(
        functools.partial(_moba_prompt_kernel, units_per_step=units_per_step, n_units=n_units),
        grid_spec=pltpu.PrefetchScalarGridSpec(
            num_scalar_prefetch=1,
            grid=(batch, N_HEADS, nq),
            in_specs=[q_spec, q_spec, kv_spec, kv_spec, kv_spec, pl.BlockSpec(memory_space=pl.ANY)],
            out_specs=[q_spec, pl.BlockSpec((1, units_per_step, N_HEADS, HEAD_DIM), bsum_index)],
            scratch_shapes=[pltpu.VMEM((nq, HEAD_DIM), F32), pltpu.VMEM((nq, HEAD_DIM, MOBA_BLOCK), BF16),
                            pltpu.VMEM((nq, tq), F32), pltpu.VMEM((2, MOBA_GROUP, MOBA_BLOCK, tq), F32),
                            pltpu.VMEM((2, units_per_step * PAGES_PER_BLOCK, PAGE_SIZE, N_HEADS, HEAD_DIM), F32),
                            pltpu.SemaphoreType.DMA((2,))]),
        out_shape=[jax.ShapeDtypeStruct((m, WIDTH), BF16),
                   jax.ShapeDtypeStruct((db, blocks_per_sample, N_HEADS, HEAD_DIM), F32)],
        compiler_params=_params("arbitrary", "arbitrary", "arbitrary"),
        name="moba_prompt",
    )(page_table.reshape(-1), qf, qb, kf, kb, vf, cache_k)


def _neg_softplus2(z2):
    return -(jnp.maximum(z2, 0.0) + jnp.log2(1.0 + jnp.exp2(-jnp.abs(z2))))


def _split_bf16(x):
    hi = x.astype(BF16)
    return hi, (x - hi.astype(F32)).astype(BF16)


def _sb_weights(z2, carry, valid, key_axis):
    tk = z2.shape[key_axis]
    nq = z2.shape[1 - key_axis]
    log_1m = _neg_softplus2(z2)
    if valid is not None:
        log_1m = jnp.where(valid, log_1m, 0.0)
    row = lax.broadcasted_iota(jnp.int32, (tk, tk), 0)
    col = lax.broadcasted_iota(jnp.int32, (tk, tk), 1)
    both = jnp.concatenate(_split_bf16(log_1m), axis=1 - key_axis)
    if key_axis == 0:
        both = jnp.dot(jnp.where(col > row, 1.0, 0.0).astype(BF16), both, preferred_element_type=F32)
        suffix = both[:, :nq] + both[:, nq:]
    else:
        both = jnp.dot(both, jnp.where(row > col, 1.0, 0.0).astype(BF16), preferred_element_type=F32)
        suffix = both[:nq] + both[nq:]
    a = jnp.exp2(z2 + log_1m + suffix + carry)
    if valid is not None:
        a = jnp.where(valid, a, 0.0)
    return a, carry + jnp.sum(log_1m, axis=key_axis, keepdims=True)


SB_TILE = 128
SB_CHAINS = 8


def _sb_prompt_kernel(q_ref, k_ref, vf_ref, o_ref, vt_ref):
    qi = pl.program_id(2)
    t = SB_TILE
    n_tiles = vt_ref.shape[0]

    @pl.when(qi == 0)
    def _():
        for n in range(n_tiles):
            vt_ref[n] = vf_ref[n * t:(n + 1) * t, :].T.astype(BF16)

    width = SB_CHAINS * t
    diag0 = qi * SB_CHAINS
    chain_of_lane = lax.broadcasted_iota(jnp.int32, (1, width), 1) // t

    def walk(j, carry, valid):
        key_tiles = [jnp.maximum(diag0 + r - j, 0) for r in range(SB_CHAINS)]
        z2 = jnp.concatenate(
            [lax.dot_general(k_ref[pl.ds(pl.multiple_of(key_tiles[r] * t, t), t), :], q_ref[r * t:(r + 1) * t, :],
                             NT_DIMS, preferred_element_type=F32) for r in range(SB_CHAINS)], axis=1)
        a, new_carry = _sb_weights(z2, carry, valid, 0)
        has_keys = chain_of_lane + diag0 >= j
        a = jnp.where(has_keys, a, 0.0).astype(BF16)
        out = jnp.concatenate([jnp.dot(vt_ref[key_tiles[r]], a[:, r * t:(r + 1) * t], preferred_element_type=F32)
                               for r in range(SB_CHAINS)], axis=1)
        new_carry = jnp.where(has_keys, new_carry, carry)
        live = jnp.max(jnp.where(chain_of_lane + diag0 >= j + 1, new_carry, SB_STOP))
        return out, new_carry, live

    key_id = lax.broadcasted_iota(jnp.int32, (t, width), 0)
    qry_id = lax.broadcasted_iota(jnp.int32, (t, width), 1) % t
    acc, carry, live = walk(jnp.int32(0), jnp.zeros((1, width), F32), key_id < qry_id)

    def more(state):
        return state[3] > SB_STOP

    def step(state):
        j, acc, carry, _ = state
        out, carry, live = walk(j, carry, None)
        return j + 1, acc + out, carry, live

    _, acc, _, _ = lax.while_loop(more, step, (jnp.int32(1), acc, carry, live))
    o_ref[...] = acc.T.astype(o_ref.dtype)


def _sb_prompt_call(qb, kb, vf, batch):
    m, _ = qb.shape
    s = m // batch
    tq = SB_TILE * SB_CHAINS
    nq = s // tq
    assert s % tq == 0
    q_spec = pl.BlockSpec((tq, HEAD_DIM), lambda b, h, i: (b * nq + i, h))
    kv_spec = pl.BlockSpec((s, HEAD_DIM), lambda b, h, i: (b, h))
    return pl.pallas_call(
        _sb_prompt_kernel,
        grid=(batch, N_HEADS, nq),
        in_specs=[q_spec, kv_spec, kv_spec],
        out_specs=q_spec,
        out_shape=jax.ShapeDtypeStruct((m, WIDTH), BF16),
        scratch_shapes=[pltpu.VMEM((s // SB_TILE, HEAD_DIM, SB_TILE), BF16)],
        compiler_params=_params("arbitrary", "arbitrary", "arbitrary"),
        name="sb_prompt",
    )(qb, kb, vf)


def _sb_sample_kernel(pt_ref, q_ref, kn_ref, vn_ref, ck_ref, cv_ref, o_ref, kbuf, vbuf, sem, *, n_tok, n_pages):
    b = pl.program_id(0)
    rows = N_HEADS * SUBLANES

    def page_copies(p, slot):
        phys = pt_ref[b * n_pages + p]
        return (pltpu.make_async_copy(ck_ref.at[phys], kbuf.at[slot], sem.at[slot, 0]),
                pltpu.make_async_copy(cv_ref.at[phys], vbuf.at[slot], sem.at[slot, 1]))

    def slot_of(p):
        return (n_pages - 1 - p) % 2

    for c in page_copies(n_pages - 1, 0):
        c.start()

    def head_q(h):
        return q_ref[0, :, h * HEAD_DIM:(h + 1) * HEAD_DIM].astype(BF16)

    def tile(k_of_head, v_of_head, carry, valid):
        z2 = jnp.concatenate([lax.dot_general(head_q(h), k_of_head(h), NT_DIMS, preferred_element_type=F32)
                              for h in range(N_HEADS)], axis=0)
        a, carry = _sb_weights(z2, carry, valid, 1)
        a = a.astype(BF16)
        out = jnp.concatenate([jnp.dot(a[h * SUBLANES:(h + 1) * SUBLANES], v_of_head(h), preferred_element_type=F32)
                               for h in range(N_HEADS)], axis=0)
        return out, carry

    tk = kn_ref.shape[1]
    key = lax.broadcasted_iota(jnp.int32, (rows, tk), 1)
    tok = lax.broadcasted_iota(jnp.int32, (rows, tk), 0) % SUBLANES
    acc, carry = tile(lambda h: kn_ref[0, :, h * HEAD_DIM:(h + 1) * HEAD_DIM],
                      lambda h: vn_ref[0, :, h * HEAD_DIM:(h + 1) * HEAD_DIM],
                      jnp.zeros((rows, 1), F32), key < tok)
    tok_of_row = lax.broadcasted_iota(jnp.int32, (rows, 1), 0) % SUBLANES

    def more(state):
        p, _, carry = state
        live = jnp.max(jnp.where(tok_of_row < n_tok, carry, SB_STOP)) > SB_STOP
        return jnp.logical_and(p >= 0, live)

    def step(state):
        p, acc, carry = state
        slot = slot_of(p)
        for c in page_copies(p, slot):
            c.wait()

        @pl.when(p > 0)
        def _():
            for c in page_copies(p - 1, 1 - slot):
                c.start()

        out, carry = tile(lambda h: kbuf[slot, :, h, :].astype(BF16), lambda h: vbuf[slot, :, h, :].astype(BF16),
                          carry, None)
        return p - 1, acc + out, carry

    p, acc, _ = lax.while_loop(more, step, (jnp.int32(n_pages - 1), acc, carry))

    @pl.when(p >= 0)
    def _():
        for c in page_copies(p, slot_of(p)):
            c.wait()

    for h in range(N_HEADS):
        o_ref[0, :, h * HEAD_DIM:(h + 1) * HEAD_DIM] = acc[h * SUBLANES:(h + 1) * SUBLANES, :].astype(o_ref.dtype)


def _sb_sample_call(q_pad, k_new, v_new, cache_k, cache_v, page_table, n_tok):
    db, rows, w = q_pad.shape
    n_pages = page_table.shape[1]
    per_b = lambda b, pt: (b, 0, 0)
    any_spec = pl.BlockSpec(memory_space=pl.ANY)
    page_buf = pltpu.VMEM((2, PAGE_SIZE, N_HEADS, HEAD_DIM), F32)
    return pl.pallas_call(
        functools.partial(_sb_sample_kernel, n_tok=n_tok, n_pages=n_pages),
        grid_spec=pltpu.PrefetchScalarGridSpec(
            num_scalar_prefetch=1,
            grid=(db,),
            in_specs=[pl.BlockSpec((1, rows, w), per_b),
                      pl.BlockSpec((1, PAGE_SIZE, w), per_b), pl.BlockSpec((1, PAGE_SIZE, w), per_b),
                      any_spec, any_spec],
            out_specs=pl.BlockSpec((1, rows, w), per_b),
            scratch_shapes=[page_buf, page_buf, pltpu.SemaphoreType.DMA((2, 2))]),
        out_shape=jax.ShapeDtypeStruct((db, rows, w), BF16),
        compiler_params=_params("arbitrary"),
        name="sb_sample",
    )(page_table.reshape(-1), q_pad, k_new, v_new, cache_k, cache_v)


def _moba_select_kernel(bsum_ref, q_ref, o_ref):
    q = q_ref[0]
    gates = [lax.dot_general(q[:, h * HEAD_DIM:(h + 1) * HEAD_DIM], bsum_ref[0, :, h, :] * (1.0 / MOBA_BLOCK),
                             NT_DIMS, precision=lax.Precision.HIGHEST, preferred_element_type=F32)
             for h in range(N_HEADS)]
    gate = jnp.concatenate(gates, axis=0)
    lane = lax.broadcasted_iota(jnp.int32, o_ref.shape[1:], 1)
    out = jnp.zeros(o_ref.shape[1:], jnp.int32)
    for r, (idx, _) in enumerate(_top_k_indices(gate, MOBA_TOPK, 1)):
        out = jnp.where(lane == r, idx, out)
    o_ref[0] = out


def _moba_select_call(block_sums, q_pad):
    db, n_blocks = block_sums.shape[:2]
    rows, w = q_pad.shape[1:]
    return pl.pallas_call(
        _moba_select_kernel,
        grid=(db,),
        in_specs=[pl.BlockSpec((1, n_blocks, N_HEADS, HEAD_DIM), lambda b: (b, 0, 0, 0)),
                  pl.BlockSpec((1, rows, w), lambda b: (b, 0, 0))],
        out_specs=pl.BlockSpec((1, N_HEADS * rows, LANES), lambda b: (b, 0, 0)),
        out_shape=jax.ShapeDtypeStruct((db, N_HEADS * rows, LANES), jnp.int32),
        compiler_params=_params("arbitrary"),
        name="moba_select",
    )(block_sums, q_pad)


def _moba_sample_kernel(sel_ref, pt_ref, q_ref, kn_ref, vn_ref, ck_ref, cv_ref, o_ref, kbuf, vbuf, sem,
                        *, n_tok, n_pages):
    step = pl.program_id(0)
    n_steps = pl.num_programs(0)
    per_tok = MOBA_TOPK * PAGES_PER_BLOCK

    def copies(s, slot):
        b = s // N_HEADS
        h = s % N_HEADS
        out = []
        for t in range(n_tok):
            for r in range(MOBA_TOPK):
                blk = sel_ref[(s * n_tok + t) * MOBA_TOPK + r]
                for j in range(PAGES_PER_BLOCK):
                    phys = pt_ref[b * n_pages + blk * PAGES_PER_BLOCK + j]
                    i = (t * MOBA_TOPK + r) * PAGES_PER_BLOCK + j
                    out.append(pltpu.make_async_copy(ck_ref.at[phys, :, h, :], kbuf.at[slot, i], sem.at[slot, 0]))
                    out.append(pltpu.make_async_copy(cv_ref.at[phys, :, h, :], vbuf.at[slot, i], sem.at[slot, 1]))
        return out

    slot = step % 2

    @pl.when(step == 0)
    def _():
        for c in copies(step, slot):
            c.start()

    @pl.when(step + 1 < n_steps)
    def _():
        for c in copies(step + 1, 1 - slot):
            c.start()

    for c in copies(step, slot):
        c.wait()

    q = q_ref[0].astype(BF16)
    row = lax.broadcasted_iota(jnp.int32, (SUBLANES, 1), 0)
    s_sel = jnp.zeros((SUBLANES, per_tok * PAGE_SIZE), F32)
    for t in range(n_tok):
        k_t = kbuf[slot, t * per_tok:(t + 1) * per_tok].reshape(per_tok * PAGE_SIZE, HEAD_DIM).astype(BF16)
        s_sel = jnp.where(row == t, lax.dot_general(q, k_t, NT_DIMS, preferred_element_type=F32), s_sel)
    s_own = lax.dot_general(q, kn_ref[0].astype(BF16), NT_DIMS, preferred_element_type=F32)
    key = lax.broadcasted_iota(jnp.int32, s_own.shape, 1)
    s_own = jnp.where(key <= row, s_own, NEG_BIG)
    m = jnp.maximum(jnp.max(s_sel, axis=-1, keepdims=True), jnp.max(s_own, axis=-1, keepdims=True))
    p_sel = jnp.exp(s_sel - m)
    p_own = jnp.exp(s_own - m)
    denom = jnp.sum(p_sel, axis=-1, keepdims=True) + jnp.sum(p_own, axis=-1, keepdims=True)
    out = jnp.dot(p_own.astype(BF16), vn_ref[0].astype(BF16), preferred_element_type=F32)
    p_sel = p_sel.astype(BF16)
    for t in range(n_tok):
        v_t = vbuf[slot, t * per_tok:(t + 1) * per_tok].reshape(per_tok * PAGE_SIZE, HEAD_DIM).astype(BF16)
        out = out + jnp.where(row == t, jnp.dot(p_sel, v_t, preferred_element_type=F32), 0.0)
    o_ref[0] = (out / denom).astype(o_ref.dtype)


def _moba_sample_call(sel, page_table, q_pad, k_pad, v_pad, cache_k, cache_v, n_tok):
    db, n_pages = page_table.shape
    rows, w = q_pad.shape[1:]
    n_sel = n_tok * MOBA_TOPK * PAGES_PER_BLOCK
    new_spec = pl.BlockSpec((1, rows, HEAD_DIM), lambda s, *_: (s // N_HEADS, 0, s % N_HEADS))
    any_spec = pl.BlockSpec(memory_space=pl.ANY)
    return pl.pallas_call(
        functools.partial(_moba_sample_kernel, n_tok=n_tok, n_pages=n_pages),
        grid_spec=pltpu.PrefetchScalarGridSpec(
            num_scalar_prefetch=2,
            grid=(db * N_HEADS,),
            in_specs=[new_spec, new_spec, new_spec, any_spec, any_spec],
            out_specs=new_spec,
            scratch_shapes=[pltpu.VMEM((2, n_sel, PAGE_SIZE, HEAD_DIM), F32),
                            pltpu.VMEM((2, n_sel, PAGE_SIZE, HEAD_DIM), F32),
                            pltpu.SemaphoreType.DMA((2, 2))]),
        out_shape=jax.ShapeDtypeStruct((db, rows, w), BF16),
        compiler_params=_params("arbitrary"),
        name="moba_sample",
    )(sel.reshape(-1), page_table.reshape(-1), q_pad, k_pad, v_pad, cache_k, cache_v)


def _rope_tables(pos):
    half = HEAD_DIM // 2
    inv = ROPE_THETA ** (-jnp.arange(half, dtype=F32) / half)
    ang = pos.astype(F32)[:, None] * inv[None, :]
    cos, sin = jnp.cos(ang), jnp.sin(ang)
    return jnp.concatenate([cos, cos], axis=-1), jnp.concatenate([-sin, sin], axis=-1)


def _split_cols(w, bounds):
    return [w[:, a:b] for a, b in zip(bounds[:-1], bounds[1:])]


def _attn_inputs(x, mods, gain, w_parts, q_gain, k_gain, cos, sin, tm):
    shift_a, scale_a = mods
    h = _norm_call(x, gain, scale_a, shift_a, tm)
    w_qm, w_km, w_vm, w_qs, w_ks, w_vs, w_gm, w_gs = w_parts
    q_scale = ATTN_SCALE * LOG2E
    qm_f, qm_b = _proj_call("rope", h, w_qm, tm, out_scale=q_scale, gain=q_gain, cos=cos, sin=sin)
    km_f, km_b = _proj_call("rope", h, w_km, tm, gain=k_gain, cos=cos, sin=sin)
    vm_f, = _proj_call("raw", h, w_vm, tm, dtypes=(F32,))
    qs_b, = _proj_call("raw", h, w_qs, tm, out_scale=q_scale, dtypes=(BF16,))
    ks_f, ks_b = _proj_call("raw", h, w_ks, tm)
    vs_f, = _proj_call("raw", h, w_vs, tm, dtypes=(F32,))
    g_m = _proj_call("sigmoid", h, w_gm, tm)
    g_s = _proj_call("sigmoid", h, w_gs, tm)
    return dict(qm_f=qm_f, qm_b=qm_b, km_f=km_f, km_b=km_b, vm_f=vm_f,
                qs_b=qs_b, ks_f=ks_f, ks_b=ks_b, vs_f=vs_f, g_m=g_m, g_s=g_s)


def kernel(x_prompt, x_sample, cache_k_moba, cache_v_moba, cache_k_sb, cache_v_sb, page_table, c_prompt, c_sample, w_ada, b_ada, attn_norm_g, w_in, q_norm_g, k_norm_g, w_br_moba, w_br_sb, w_out, mlp_norm_g, w_ff1, w_ff2):
    batch, seq, d = x_prompt.shape
    db, n_tok, _ = x_sample.shape
    depth, n_phys = cache_k_moba.shape[:2]
    n_pages = page_table.shape[1]
    past_len = n_pages * PAGE_SIZE
    assert past_len % MOBA_BLOCK == 0 and past_len // MOBA_BLOCK >= MOBA_TOPK and n_tok <= SUBLANES
    m_p, m_s = batch * seq, db * n_tok
    tm_p, tm_s = 512, m_s
    assert m_s % SUBLANES == 0 and seq % tm_p == 0

    cos_p, sin_p = _rope_tables(jnp.arange(seq, dtype=jnp.int32))
    cos_s, sin_s = _rope_tables(past_len + jnp.arange(n_tok, dtype=jnp.int32))
    cos_s, sin_s = jnp.tile(cos_s, (db, 1)), jnp.tile(sin_s, (db, 1))

    n_c = batch + db
    c_rows = -(-n_c // SUBLANES) * SUBLANES
    c_all = jnp.pad(jnp.concatenate([c_prompt, c_sample], axis=0), ((0, c_rows - n_c), (0, 0)))
    bounds = [0, WIDTH, 2 * WIDTH, 3 * WIDTH, 4 * WIDTH, 5 * WIDTH, 6 * WIDTH, 6 * WIDTH + d, 6 * WIDTH + 2 * d]
    cache_shape = (depth * n_phys, PAGE_SIZE, N_HEADS, HEAD_DIM)
    ck_m, cv_m = cache_k_moba.reshape(cache_shape), cache_v_moba.reshape(cache_shape)
    ck_s, cv_s = cache_k_sb.reshape(cache_shape), cache_v_sb.reshape(cache_shape)

    def pad_rows(v, rows):
        return jnp.pad(v.reshape(db, n_tok, WIDTH), ((0, 0), (0, rows - n_tok), (0, 0)))

    xp = x_prompt.reshape(m_p, d)
    xs = x_sample.reshape(m_s, d)
    outs = [[] for _ in range(8)]
    for l in range(depth):
        mods = _ada_mods(c_all, w_ada[l], b_ada[l])
        mods_p = [mods[:batch, i * d:(i + 1) * d].reshape(batch, 1, d) for i in range(N_MOD)]
        mods_s = [jnp.repeat(mods[batch:n_c, i * d:(i + 1) * d], n_tok, axis=0).reshape(1, m_s, d)
                  for i in range(N_MOD)]
        w_parts = _split_cols(w_in[l].astype(BF16), bounds)
        w_m, w_s, w_o = w_br_moba[l].astype(BF16), w_br_sb[l].astype(BF16), w_out[l].astype(BF16)
        w_1, w_2 = w_ff1[l].astype(BF16), w_ff2[l].astype(BF16)

        pt = page_table + l * n_phys
        a = _attn_inputs(xp, mods_p[:2], attn_norm_g[l], w_parts, q_norm_g[l], k_norm_g[l], cos_p, sin_p, tm_p)
        o_m, block_sums = _moba_prompt_call(a["qm_f"], a["qm_b"], a["km_f"], a["km_b"], a["vm_f"], batch, ck_m, pt)
        o_s = _sb_prompt_call(a["qs_b"], a["ks_b"], a["vs_f"], batch)
        x1, h2 = _merge_call(o_m, o_s, a["g_m"], a["g_s"], xp, w_m, w_s, w_o, mods_p[2], mlp_norm_g[l],
                             mods_p[4], mods_p[3], 256)
        xp = _ffn_call(h2, w_1, w_2, x1, mods_p[5], tm_p, 1024)
        for dst, key in zip(outs[:4], ("km_f", "vm_f", "ks_f", "vs_f")):
            dst.append(a[key].reshape(batch, seq, N_HEADS, HEAD_DIM))

        a = _attn_inputs(xs, mods_s[:2], attn_norm_g[l], w_parts, q_norm_g[l], k_norm_g[l], cos_s, sin_s, tm_s)
        qm_pad = pad_rows(a["qm_f"], SUBLANES)
        picks = _moba_select_call(block_sums, qm_pad)
        sel = picks.reshape(db, N_HEADS, SUBLANES, LANES)[:, :, :n_tok, :MOBA_TOPK]
        o_m = _moba_sample_call(sel, pt, qm_pad * ATTN_SCALE, pad_rows(a["km_f"], SUBLANES),
                                pad_rows(a["vm_f"], SUBLANES), ck_m, cv_m, n_tok)
        o_m = o_m[:, :n_tok].reshape(m_s, WIDTH)
        o_s = _sb_sample_call(pad_rows(a["qs_b"], SUBLANES).astype(F32), pad_rows(a["ks_b"], PAGE_SIZE),
                              pad_rows(a["vs_f"], PAGE_SIZE).astype(BF16), ck_s, cv_s, pt, n_tok)
        o_s = o_s[:, :n_tok].reshape(m_s, WIDTH)
        x1, h2 = _merge_call(o_m, o_s, a["g_m"], a["g_s"], xs, w_m, w_s, w_o, mods_s[2], mlp_norm_g[l],
                             mods_s[4], mods_s[3], tm_s)
        xs = _ffn_call(h2, w_1, w_2, x1, mods_s[5], tm_s, 1024)
        for dst, key in zip(outs[4:], ("km_f", "vm_f", "ks_f", "vs_f")):
            dst.append(a[key].reshape(db, n_tok, N_HEADS, HEAD_DIM))

    return (xp.reshape(batch, seq, d), xs.reshape(db, n_tok, d), *[jnp.stack(o) for o in outs])
```

```python
import functools

import numpy as np
import jax
import jax.numpy as jnp
from jax import lax
from jax.experimental import pallas as pl
from jax.experimental.pallas import tpu as pltpu

HEAD_DIM = 128
N_HEADS = 8
WIDTH = N_HEADS * HEAD_DIM
MOBA_BLOCK = 256
MOBA_TOPK = 3
PAGE_SIZE = 128
PAGES_PER_BLOCK = MOBA_BLOCK // PAGE_SIZE
ROPE_THETA = 10000.0
NORM_EPS = 1e-6
N_MOD = 6
ATTN_SCALE = HEAD_DIM ** -0.5
LOG2E = 1.4426950408889634

LANES = 128
SUBLANES = 8
VMEM_LIMIT_BYTES = 56 * 1024 * 1024
NEG_BIG = -1e30
SB_STOP = -120.0 * LOG2E

F32 = jnp.float32
BF16 = jnp.bfloat16
NT_DIMS = (((1,), (1,)), ((), ()))


def _params(*sem):
    return pltpu.CompilerParams(dimension_semantics=sem, vmem_limit_bytes=VMEM_LIMIT_BYTES)


def _mods_kernel(c_ref, w_ref, b_ref, o_ref):
    o_ref[...] = jnp.dot(c_ref[...], w_ref[...], precision=lax.Precision.HIGHEST,
                         preferred_element_type=F32) + b_ref[...]


def _ada_mods(c, w, b):
    r, d = c.shape
    n = w.shape[1]
    tn = 512
    return pl.pallas_call(
        _mods_kernel,
        grid=(n // tn,),
        in_specs=[pl.BlockSpec((r, d), lambda j: (0, 0)),
                  pl.BlockSpec((d, tn), lambda j: (0, j)),
                  pl.BlockSpec((1, tn), lambda j: (0, j))],
        out_specs=pl.BlockSpec((r, tn), lambda j: (0, j)),
        out_shape=jax.ShapeDtypeStruct((r, n), F32),
        compiler_params=_params("arbitrary"),
        name="ada_mods",
    )(c, w, b.reshape(1, n))


def _norm_mod(x, gain, scale, shift):
    ms = jnp.mean(x * x, axis=-1, keepdims=True)
    return (x * lax.rsqrt(ms + NORM_EPS) * gain) * (1.0 + scale) + shift


def _norm_mod_kernel(x_ref, g_ref, sc_ref, sh_ref, o_ref):
    o_ref[...] = _norm_mod(x_ref[...], g_ref[...], sc_ref[0], sh_ref[0]).astype(o_ref.dtype)


def _mod_spec(mod, m, tm):
    g, r, d = mod.shape
    tiles_per_group = (m // g) // tm
    return pl.BlockSpec((1, r, d), lambda i, *_: (i // tiles_per_group, 0, 0))


def _norm_call(x, gain, scale, shift, tm):
    m, d = x.shape
    return pl.pallas_call(
        _norm_mod_kernel,
        grid=(m // tm,),
        in_specs=[pl.BlockSpec((tm, d), lambda i: (i, 0)),
                  pl.BlockSpec((1, d), lambda i: (0, 0)),
                  _mod_spec(scale, m, tm), _mod_spec(shift, m, tm)],
        out_specs=pl.BlockSpec((tm, d), lambda i: (i, 0)),
        out_shape=jax.ShapeDtypeStruct((m, d), BF16),
        compiler_params=_params("arbitrary"),
        name="norm_mod",
    )(x, gain.reshape(1, d), scale, shift)


def _proj_rope_kernel(h_ref, w_ref, g_ref, cos_ref, sin_ref, of_ref, ob_ref, *, out_scale):
    z = jnp.dot(h_ref[...], w_ref[...], preferred_element_type=F32)
    g = g_ref[...]
    cos = cos_ref[...]
    sin = sin_ref[...]
    for hd in range(z.shape[1] // HEAD_DIM):
        sl = slice(hd * HEAD_DIM, (hd + 1) * HEAD_DIM)
        zh = z[:, sl]
        y = zh * lax.rsqrt(jnp.mean(zh * zh, axis=-1, keepdims=True) + NORM_EPS) * g
        y = y * cos + pltpu.roll(y, HEAD_DIM // 2, 1) * sin
        of_ref[:, sl] = y
        ob_ref[:, sl] = (y * out_scale).astype(ob_ref.dtype)


def _proj_raw_kernel(h_ref, w_ref, *o_refs, out_scale):
    z = jnp.dot(h_ref[...], w_ref[...], preferred_element_type=F32)
    for o_ref in o_refs:
        o_ref[...] = z if o_ref.dtype == F32 else (z * out_scale).astype(o_ref.dtype)


def _proj_sigmoid_kernel(h_ref, w_ref, o_ref):
    z = jnp.dot(h_ref[...], w_ref[...], preferred_element_type=F32)
    o_ref[...] = 1.0 / (1.0 + jnp.exp(-z))


def _proj_call(kind, h, w_cols, tm, *, out_scale=1.0, gain=None, cos=None, sin=None, dtypes=(F32, BF16)):
    m, d = h.shape
    w, first, n = w_cols
    assert first % n == 0
    row = lambda i: (i, 0)
    h_spec = pl.BlockSpec((tm, d), row)
    w_spec = pl.BlockSpec((d, n), lambda i: (0, first // n))
    o_spec = pl.BlockSpec((tm, n), row)
    two_out = dict(out_specs=[o_spec, o_spec],
                   out_shape=[jax.ShapeDtypeStruct((m, n), F32), jax.ShapeDtypeStruct((m, n), BF16)])
    if kind == "rope":
        tab_tiles = cos.shape[0] // tm
        tab_spec = pl.BlockSpec((tm, HEAD_DIM), lambda i: (i % tab_tiles, 0))
        return pl.pallas_call(
            functools.partial(_proj_rope_kernel, out_scale=out_scale),
            grid=(m // tm,),
            in_specs=[h_spec, w_spec, pl.BlockSpec((1, HEAD_DIM), lambda i: (0, 0)), tab_spec, tab_spec],
            compiler_params=_params("arbitrary"), name="proj_rope", **two_out,
        )(h, w, gain.reshape(1, HEAD_DIM), cos, sin)
    if kind == "raw":
        return pl.pallas_call(
            functools.partial(_proj_raw_kernel, out_scale=out_scale),
            grid=(m // tm,), in_specs=[h_spec, w_spec],
            out_specs=[o_spec] * len(dtypes),
            out_shape=[jax.ShapeDtypeStruct((m, n), dt) for dt in dtypes],
            compiler_params=_params("arbitrary"), name="proj_raw",
        )(h, w)
    assert kind == "sigmoid"
    return pl.pallas_call(
        _proj_sigmoid_kernel,
        grid=(m // tm,), in_specs=[h_spec, w_spec], out_specs=o_spec,
        out_shape=jax.ShapeDtypeStruct((m, n), F32),
        compiler_params=_params("arbitrary"), name="proj_sigmoid",
    )(h, w)


def _merge_kernel(om_ref, os_ref, gm_ref, gs_ref, x_ref, wm_ref, ws_ref, wo_ref,
                  ga_ref, g2_ref, sc_ref, sh_ref, x1_ref, h2_ref):
    u = (gm_ref[...] * jnp.dot(om_ref[...], wm_ref[...], preferred_element_type=F32)
         + gs_ref[...] * jnp.dot(os_ref[...], ws_ref[...], preferred_element_type=F32))
    x1 = x_ref[...] + ga_ref[0] * jnp.dot(u.astype(BF16), wo_ref[...], preferred_element_type=F32)
    x1_ref[...] = x1
    h2_ref[...] = _norm_mod(x1, g2_ref[...], sc_ref[0], sh_ref[0]).astype(h2_ref.dtype)


def _merge_call(o_m, o_s, g_m, g_s, x, w_m, w_s, w_o, gate_a, gain2, scale_f, shift_f, tm):
    m, d = x.shape
    wd = o_m.shape[1]
    row = lambda i: (i, 0)
    const = lambda i: (0, 0)
    resident = functools.partial(pl.BlockSpec, index_map=const, pipeline_mode=pl.Buffered(1))
    return pl.pallas_call(
        _merge_kernel,
        grid=(m // tm,),
        in_specs=[pl.BlockSpec((tm, wd), row), pl.BlockSpec((tm, wd), row),
                  pl.BlockSpec((tm, d), row), pl.BlockSpec((tm, d), row), pl.BlockSpec((tm, d), row),
                  resident((wd, d)), resident((wd, d)), resident((d, d)),
                  _mod_spec(gate_a, m, tm), pl.BlockSpec((1, d), const),
                  _mod_spec(scale_f, m, tm), _mod_spec(shift_f, m, tm)],
        out_specs=[pl.BlockSpec((tm, d), row), pl.BlockSpec((tm, d), row)],
        out_shape=[jax.ShapeDtypeStruct((m, d), F32), jax.ShapeDtypeStruct((m, d), BF16)],
        compiler_params=_params("arbitrary"),
        name="merge",
    )(o_m, o_s, g_m, g_s, x, w_m, w_s, w_o, gate_a, gain2.reshape(1, d), scale_f, shift_f)


def _ffn_kernel(h_ref, w1_ref, w2_ref, x_ref, gf_ref, o_ref, acc_ref):
    k = pl.program_id(1)

    @pl.when(k == 0)
    def _():
        acc_ref[...] = jnp.zeros_like(acc_ref)

    a = jnp.dot(h_ref[...], w1_ref[...], preferred_element_type=F32)
    a = jnp.square(jnp.maximum(a, 0.0)).astype(BF16)
    acc_ref[...] += jnp.dot(a, w2_ref[...], preferred_element_type=F32)

    @pl.when(k == pl.num_programs(1) - 1)
    def _():
        o_ref[...] = x_ref[...] + gf_ref[0] * acc_ref[...]


def _ffn_call(h2, w1, w2, x1, gate_f, tm, tf):
    m, d = x1.shape
    f = w1.shape[1]
    return pl.pallas_call(
        _ffn_kernel,
        grid=(m // tm, f // tf),
        in_specs=[pl.BlockSpec((tm, d), lambda i, k: (i, 0)),
                  pl.BlockSpec((d, tf), lambda i, k: (0, k)),
                  pl.BlockSpec((tf, d), lambda i, k: (k, 0)),
                  pl.BlockSpec((tm, d), lambda i, k: (i, 0)),
                  _mod_spec(gate_f, m, tm)],
        out_specs=pl.BlockSpec((tm, d), lambda i, k: (i, 0)),
        out_shape=jax.ShapeDtypeStruct((m, d), F32),
        scratch_shapes=[pltpu.VMEM((tm, d), F32)],
        compiler_params=_params("arbitrary", "arbitrary"),
        name="ffn",
    )(h2, w1, w2, x1, gate_f)


def _top_k_indices(gate, k, axis):
    idx = lax.broadcasted_iota(jnp.int32, gate.shape, axis)
    n = gate.shape[axis]
    g = gate
    out = []
    for _ in range(k):
        best = jnp.max(g, axis=axis, keepdims=True)
        first = jnp.min(jnp.where(g == best, idx, n), axis=axis, keepdims=True)
        out.append((first, best))
        g = jnp.where(idx == first, -jnp.inf, g)
    return out


def _top_k_mask(gate, n_valid, k, axis):
    idx = lax.broadcasted_iota(jnp.int32, gate.shape, axis)
    sel = jnp.zeros(gate.shape, jnp.bool_)
    for first, best in _top_k_indices(jnp.where(idx < n_valid, gate, -jnp.inf), k, axis):
        sel = jnp.logical_or(sel, jnp.logical_and(idx == first, best > -jnp.inf))
    return sel


MOBA_GROUP = 4


def _moba_prompt_kernel(pt_ref, qf_ref, qb_ref, kf_ref, kb_ref, vf_ref, ck_ref, o_ref, bsum_ref,
                        kmean_ref, vt_ref, bias_ref, s_ref, pbuf, psem, *, units_per_step, n_units):
    qi = pl.program_id(2)
    blk = MOBA_BLOCK
    n_blocks = kmean_ref.shape[0]
    step = (pl.program_id(0) * pl.num_programs(1) + pl.program_id(1)) * pl.num_programs(2) + qi
    stream_steps = n_units // units_per_step
    pages_per_step = units_per_step * PAGES_PER_BLOCK
    slot_p = step % 2

    def stream_copies(s, slot):
        return [pltpu.make_async_copy(ck_ref.at[pt_ref[s * pages_per_step + j]], pbuf.at[slot, j], psem.at[slot])
                for j in range(pages_per_step)]

    @pl.when(step == 0)
    def _():
        for c in stream_copies(step, slot_p):
            c.start()

    @pl.when(step + 1 < stream_steps)
    def _():
        for c in stream_copies(step + 1, 1 - slot_p):
            c.start()

    @pl.when(qi == 0)
    def _():
        for n in range(n_blocks):
            rows = slice(n * blk, (n + 1) * blk)
            kmean_ref[n:n + 1, :] = jnp.sum(kf_ref[rows, :], axis=0, keepdims=True) * (1.0 / blk)
            vt_ref[n] = vf_ref[rows, :].T.astype(BF16)

    gate = lax.dot_general(kmean_ref[...], qf_ref[...], NT_DIMS, precision=lax.Precision.HIGHEST,
                           preferred_element_type=F32)
    bias_ref[...] = jnp.where(_top_k_mask(gate, qi, MOBA_TOPK, 0), 0.0, NEG_BIG)

    q = qb_ref[...]

    def scores(n):
        start = pl.multiple_of(n * blk, blk)
        return lax.dot_general(kb_ref[pl.ds(start, blk), :], q, NT_DIMS, preferred_element_type=F32)

    s = scores(qi)
    key_id = lax.broadcasted_iota(jnp.int32, s.shape, 0)
    qry_id = lax.broadcasted_iota(jnp.int32, s.shape, 1)
    s = jnp.where(key_id <= qry_id, s, NEG_BIG)
    m0 = jnp.max(s, axis=0, keepdims=True)
    p = jnp.exp2(s - m0)
    l0 = jnp.sum(p, axis=0, keepdims=True)
    acc0 = jnp.dot(vt_ref[qi], p.astype(BF16), preferred_element_type=F32)

    last_group = n_blocks // MOBA_GROUP - 1

    def produce(g, slot):
        mx = None
        for i in range(MOBA_GROUP):
            n = g * MOBA_GROUP + i
            s_n = scores(n) + bias_ref[pl.ds(n, 1), :]
            s_ref[slot, i] = s_n
            mx_n = jnp.max(s_n, axis=0, keepdims=True)
            mx = mx_n if mx is None else jnp.maximum(mx, mx_n)
        return mx

    def past_group(g, carry):
        m, l, acc, mx = carry
        slot = g % 2
        m_new = jnp.maximum(m, mx)
        alpha = jnp.exp2(m - m_new)
        l = alpha * l
        acc = alpha * acc
        for i in range(MOBA_GROUP):
            p = jnp.exp2(s_ref[slot, i] - m_new)
            l = l + jnp.sum(p, axis=0, keepdims=True)
            acc = acc + jnp.dot(vt_ref[g * MOBA_GROUP + i], p.astype(BF16), preferred_element_type=F32)
        mx_next = produce(jnp.minimum(g + 1, last_group), 1 - slot)
        return m_new, l, acc, mx_next

    n_groups = lax.div(qi + (MOBA_GROUP - 1), MOBA_GROUP)
    _, l, acc, _ = lax.fori_loop(0, n_groups, past_group, (m0, l0, acc0, produce(0, 0)))
    o_ref[...] = (acc / l).T.astype(o_ref.dtype)

    @pl.when(step < stream_steps)
    def _():
        for c in stream_copies(step, slot_p):
            c.wait()
        for u in range(units_per_step):
            x = pbuf[slot_p, u * PAGES_PER_BLOCK]
            for j in range(1, PAGES_PER_BLOCK):
                x = x + pbuf[slot_p, u * PAGES_PER_BLOCK + j]
            while x.shape[0] > 1:
                half = x.shape[0] // 2
                x = x[:half] + x[half:]
            bsum_ref[0, u] = x[0]


def _moba_prompt_call(qf, qb, kf, kb, vf, batch, cache_k, page_table):
    m, _ = qf.shape
    s = m // batch
    tq = MOBA_BLOCK
    nq = s // tq
    assert s % MOBA_BLOCK == 0 and nq % MOBA_GROUP == 0
    db, n_pages = page_table.shape
    blocks_per_sample = n_pages // PAGES_PER_BLOCK
    n_units = db * blocks_per_sample
    n_steps = batch * N_HEADS * nq
    units_per_step = next(u for u in range(1, blocks_per_sample + 1)
                          if blocks_per_sample % u == 0 and u * n_steps >= n_units)
    last_unit = n_units - units_per_step

    def bsum_index(b, h, i, pt):
        unit = jnp.minimum(((b * N_HEADS + h) * nq + i) * units_per_step, last_unit)
        return (unit // blocks_per_sample, (unit % blocks_per_sample) // units_per_step, 0, 0)

    q_spec = pl.BlockSpec((tq, HEAD_DIM), lambda b, h, i, pt: (b * nq + i, h))
    kv_spec = pl.BlockSpec((s, HEAD_DIM), lambda b, h, i, pt: (b, h))
    return pl.pallas_call(
        functools.partial(_moba_prompt_kernel, units_per_step=units_per_step, n_units=n_units),
        grid_spec=pltpu.PrefetchScalarGridSpec(
            num_scalar_prefetch=1,
            grid=(batch, N_HEADS, nq),
            in_specs=[q_spec, q_spec, kv_spec, kv_spec, kv_spec, pl.BlockSpec(memory_space=pl.ANY)],
            out_specs=[q_spec, pl.BlockSpec((1, units_per_step, N_HEADS, HEAD_DIM), bsum_index)],
            scratch_shapes=[pltpu.VMEM((nq, HEAD_DIM), F32), pltpu.VMEM((nq, HEAD_DIM, MOBA_BLOCK), BF16),
                            pltpu.VMEM((nq, tq), F32), pltpu.VMEM((2, MOBA_GROUP, MOBA_BLOCK, tq), F32),
                            pltpu.VMEM((2, units_per_step * PAGES_PER_BLOCK, PAGE_SIZE, N_HEADS, HEAD_DIM), F32),
                            pltpu.SemaphoreType.DMA((2,))]),
        out_shape=[jax.ShapeDtypeStruct((m, WIDTH), BF16),
                   jax.ShapeDtypeStruct((db, blocks_per_sample, N_HEADS, HEAD_DIM), F32)],
        compiler_params=_params("arbitrary", "arbitrary", "arbitrary"),
        name="moba_prompt",
    )(page_table.reshape(-1), qf, qb, kf, kb, vf, cache_k)


def _neg_softplus2(z2):
    return -(jnp.maximum(z2, 0.0) + jnp.log2(1.0 + jnp.exp2(-jnp.abs(z2))))


def _split_bf16(x):
    hi = x.astype(BF16)
    return hi, (x - hi.astype(F32)).astype(BF16)


def _sb_weights(z2, carry, valid, key_axis):
    tk = z2.shape[key_axis]
    nq = z2.shape[1 - key_axis]
    log_1m = _neg_softplus2(z2)
    if valid is not None:
        log_1m = jnp.where(valid, log_1m, 0.0)
    row = lax.broadcasted_iota(jnp.int32, (tk, tk), 0)
    col = lax.broadcasted_iota(jnp.int32, (tk, tk), 1)
    both = jnp.concatenate(_split_bf16(log_1m), axis=1 - key_axis)
    if key_axis == 0:
        both = jnp.dot(jnp.where(col > row, 1.0, 0.0).astype(BF16), both, preferred_element_type=F32)
        suffix = both[:, :nq] + both[:, nq:]
    else:
        both = jnp.dot(both, jnp.where(row > col, 1.0, 0.0).astype(BF16), preferred_element_type=F32)
        suffix = both[:nq] + both[nq:]
    a = jnp.exp2(z2 + log_1m + suffix + carry)
    if valid is not None:
        a = jnp.where(valid, a, 0.0)
    return a, carry + jnp.sum(log_1m, axis=key_axis, keepdims=True)


SB_TILE = 128
SB_CHAINS = 8


def _sb_prompt_kernel(q_ref, k_ref, vf_ref, o_ref, vt_ref):
    qi = pl.program_id(2)
    t = SB_TILE
    n_tiles = vt_ref.shape[0]

    @pl.when(qi == 0)
    def _():
        for n in range(n_tiles):
            vt_ref[n] = vf_ref[n * t:(n + 1) * t, :].T.astype(BF16)

    width = SB_CHAINS * t
    diag0 = qi * SB_CHAINS
    chain_of_lane = lax.broadcasted_iota(jnp.int32, (1, width), 1) // t

    def walk(j, carry, valid):
        key_tiles = [jnp.maximum(diag0 + r - j, 0) for r in range(SB_CHAINS)]
        z2 = jnp.concatenate(
            [lax.dot_general(k_ref[pl.ds(pl.multiple_of(key_tiles[r] * t, t), t), :], q_ref[r * t:(r + 1) * t, :],
                             NT_DIMS, preferred_element_type=F32) for r in range(SB_CHAINS)], axis=1)
        a, new_carry = _sb_weights(z2, carry, valid, 0)
        has_keys = chain_of_lane + diag0 >= j
        a = jnp.where(has_keys, a, 0.0).astype(BF16)
        out = jnp.concatenate([jnp.dot(vt_ref[key_tiles[r]], a[:, r * t:(r + 1) * t], preferred_element_type=F32)
                               for r in range(SB_CHAINS)], axis=1)
        new_carry = jnp.where(has_keys, new_carry, carry)
        live = jnp.max(jnp.where(chain_of_lane + diag0 >= j + 1, new_carry, SB_STOP))
        return out, new_carry, live

    key_id = lax.broadcasted_iota(jnp.int32, (t, width), 0)
    qry_id = lax.broadcasted_iota(jnp.int32, (t, width), 1) % t
    acc, carry, live = walk(jnp.int32(0), jnp.zeros((1, width), F32), key_id < qry_id)

    def more(state):
        return state[3] > SB_STOP

    def step(state):
        j, acc, carry, _ = state
        out, carry, live = walk(j, carry, None)
        return j + 1, acc + out, carry, live

    _, acc, _, _ = lax.while_loop(more, step, (jnp.int32(1), acc, carry, live))
    o_ref[...] = acc.T.astype(o_ref.dtype)


def _sb_prompt_call(qb, kb, vf, batch):
    m, _ = qb.shape
    s = m // batch
    tq = SB_TILE * SB_CHAINS
    nq = s // tq
    assert s % tq == 0
    q_spec = pl.BlockSpec((tq, HEAD_DIM), lambda b, h, i: (b * nq + i, h))
    kv_spec = pl.BlockSpec((s, HEAD_DIM), lambda b, h, i: (b, h))
    return pl.pallas_call(
        _sb_prompt_kernel,
        grid=(batch, N_HEADS, nq),
        in_specs=[q_spec, kv_spec, kv_spec],
        out_specs=q_spec,
        out_shape=jax.ShapeDtypeStruct((m, WIDTH), BF16),
        scratch_shapes=[pltpu.VMEM((s // SB_TILE, HEAD_DIM, SB_TILE), BF16)],
        compiler_params=_params("arbitrary", "arbitrary", "arbitrary"),
        name="sb_prompt",
    )(qb, kb, vf)


def _sb_sample_kernel(pt_ref, q_ref, kn_ref, vn_ref, ck_ref, cv_ref, o_ref, kbuf, vbuf, sem, *, n_tok, n_pages):
    b = pl.program_id(0)
    rows = N_HEADS * SUBLANES

    def page_copies(p, slot):
        phys = pt_ref[b * n_pages + p]
        return (pltpu.make_async_copy(ck_ref.at[phys], kbuf.at[slot], sem.at[slot, 0]),
                pltpu.make_async_copy(cv_ref.at[phys], vbuf.at[slot], sem.at[slot, 1]))

    def slot_of(p):
        return (n_pages - 1 - p) % 2

    for c in page_copies(n_pages - 1, 0):
        c.start()

    def head_q(h):
        return q_ref[0, :, h * HEAD_DIM:(h + 1) * HEAD_DIM].astype(BF16)

    def tile(k_of_head, v_of_head, carry, valid):
        z2 = jnp.concatenate([lax.dot_general(head_q(h), k_of_head(h), NT_DIMS, preferred_element_type=F32)
                              for h in range(N_HEADS)], axis=0)
        a, carry = _sb_weights(z2, carry, valid, 1)
        a = a.astype(BF16)
        out = jnp.concatenate([jnp.dot(a[h * SUBLANES:(h + 1) * SUBLANES], v_of_head(h), preferred_element_type=F32)
                               for h in range(N_HEADS)], axis=0)
        return out, carry

    tk = kn_ref.shape[1]
    key = lax.broadcasted_iota(jnp.int32, (rows, tk), 1)
    tok = lax.broadcasted_iota(jnp.int32, (rows, tk), 0) % SUBLANES
    acc, carry = tile(lambda h: kn_ref[0, :, h * HEAD_DIM:(h + 1) * HEAD_DIM],
                      lambda h: vn_ref[0, :, h * HEAD_DIM:(h + 1) * HEAD_DIM],
                      jnp.zeros((rows, 1), F32), key < tok)
    tok_of_row = lax.broadcasted_iota(jnp.int32, (rows, 1), 0) % SUBLANES

    def more(state):
        p, _, carry = state
        live = jnp.max(jnp.where(tok_of_row < n_tok, carry, SB_STOP)) > SB_STOP
        return jnp.logical_and(p >= 0, live)

    def step(state):
        p, acc, carry = state
        slot = slot_of(p)
        for c in page_copies(p, slot):
            c.wait()

        @pl.when(p > 0)
        def _():
            for c in page_copies(p - 1, 1 - slot):
                c.start()

        out, carry = tile(lambda h: kbuf[slot, :, h, :].astype(BF16), lambda h: vbuf[slot, :, h, :].astype(BF16),
                          carry, None)
        return p - 1, acc + out, carry

    p, acc, _ = lax.while_loop(more, step, (jnp.int32(n_pages - 1), acc, carry))

    @pl.when(p >= 0)
    def _():
        for c in page_copies(p, slot_of(p)):
            c.wait()

    for h in range(N_HEADS):
        o_ref[0, :, h * HEAD_DIM:(h + 1) * HEAD_DIM] = acc[h * SUBLANES:(h + 1) * SUBLANES, :].astype(o_ref.dtype)


def _sb_sample_call(q_pad, k_new, v_new, cache_k, cache_v, page_table, n_tok):
    db, rows, w = q_pad.shape
    n_pages = page_table.shape[1]
    per_b = lambda b, pt: (b, 0, 0)
    any_spec = pl.BlockSpec(memory_space=pl.ANY)
    page_buf = pltpu.VMEM((2, PAGE_SIZE, N_HEADS, HEAD_DIM), F32)
    return pl.pallas_call(
        functools.partial(_sb_sample_kernel, n_tok=n_tok, n_pages=n_pages),
        grid_spec=pltpu.PrefetchScalarGridSpec(
            num_scalar_prefetch=1,
            grid=(db,),
            in_specs=[pl.BlockSpec((1, rows, w), per_b),
                      pl.BlockSpec((1, PAGE_SIZE, w), per_b), pl.BlockSpec((1, PAGE_SIZE, w), per_b),
                      any_spec, any_spec],
            out_specs=pl.BlockSpec((1, rows, w), per_b),
            scratch_shapes=[page_buf, page_buf, pltpu.SemaphoreType.DMA((2, 2))]),
        out_shape=jax.ShapeDtypeStruct((db, rows, w), BF16),
        compiler_params=_params("arbitrary"),
        name="sb_sample",
    )(page_table.reshape(-1), q_pad, k_new, v_new, cache_k, cache_v)


def _moba_select_kernel(bsum_ref, q_ref, o_ref):
    q = q_ref[0]
    gates = [lax.dot_general(q[:, h * HEAD_DIM:(h + 1) * HEAD_DIM], bsum_ref[0, :, h, :] * (1.0 / MOBA_BLOCK),
                             NT_DIMS, precision=lax.Precision.HIGHEST, preferred_element_type=F32)
             for h in range(N_HEADS)]
    gate = jnp.concatenate(gates, axis=0)
    lane = lax.broadcasted_iota(jnp.int32, o_ref.shape[1:], 1)
    out = jnp.zeros(o_ref.shape[1:], jnp.int32)
    for r, (idx, _) in enumerate(_top_k_indices(gate, MOBA_TOPK, 1)):
        out = jnp.where(lane == r, idx, out)
    o_ref[0] = out


def _moba_select_call(block_sums, q_pad):
    db, n_blocks = block_sums.shape[:2]
    rows, w = q_pad.shape[1:]
    return pl.pallas_call(
        _moba_select_kernel,
        grid=(db,),
        in_specs=[pl.BlockSpec((1, n_blocks, N_HEADS, HEAD_DIM), lambda b: (b, 0, 0, 0)),
                  pl.BlockSpec((1, rows, w), lambda b: (b, 0, 0))],
        out_specs=pl.BlockSpec((1, N_HEADS * rows, LANES), lambda b: (b, 0, 0)),
        out_shape=jax.ShapeDtypeStruct((db, N_HEADS * rows, LANES), jnp.int32),
        compiler_params=_params("arbitrary"),
        name="moba_select",
    )(block_sums, q_pad)


def _moba_sample_kernel(sel_ref, pt_ref, q_ref, kn_ref, vn_ref, ck_ref, cv_ref, o_ref, kbuf, vbuf, sem,
                        *, n_tok, n_pages):
    step = pl.program_id(0)
    n_steps = pl.num_programs(0)
    per_tok = MOBA_TOPK * PAGES_PER_BLOCK

    def copies(s, slot):
        b = s // N_HEADS
        h = s % N_HEADS
        out = []
        for t in range(n_tok):
            for r in range(MOBA_TOPK):
                blk = sel_ref[(s * n_tok + t) * MOBA_TOPK + r]
                for j in range(PAGES_PER_BLOCK):
                    phys = pt_ref[b * n_pages + blk * PAGES_PER_BLOCK + j]
                    i = (t * MOBA_TOPK + r) * PAGES_PER_BLOCK + j
                    out.append(pltpu.make_async_copy(ck_ref.at[phys, :, h, :], kbuf.at[slot, i], sem.at[slot, 0]))
                    out.append(pltpu.make_async_copy(cv_ref.at[phys, :, h, :], vbuf.at[slot, i], sem.at[slot, 1]))
        return out

    slot = step % 2

    @pl.when(step == 0)
    def _():
        for c in copies(step, slot):
            c.start()

    @pl.when(step + 1 < n_steps)
    def _():
        for c in copies(step + 1, 1 - slot):
            c.start()

    pltpu.make_async_copy(kbuf.at[slot], kbuf.at[slot], sem.at[slot, 0]).wait()
    pltpu.make_async_copy(vbuf.at[slot], vbuf.at[slot], sem.at[slot, 1]).wait()

    q = q_ref[0].astype(BF16)
    row = lax.broadcasted_iota(jnp.int32, (SUBLANES, 1), 0)
    s_sel = jnp.zeros((SUBLANES, per_tok * PAGE_SIZE), F32)
    for t in range(n_tok):
        k_t = kbuf[slot, t * per_tok:(t + 1) * per_tok].reshape(per_tok * PAGE_SIZE, HEAD_DIM).astype(BF16)
        s_sel = jnp.where(row == t, lax.dot_general(q, k_t, NT_DIMS, preferred_element_type=F32), s_sel)
    s_own = lax.dot_general(q, kn_ref[0].astype(BF16), NT_DIMS, preferred_element_type=F32)
    key = lax.broadcasted_iota(jnp.int32, s_own.shape, 1)
    s_own = jnp.where(key <= row, s_own, NEG_BIG)
    m = jnp.maximum(jnp.max(s_sel, axis=-1, keepdims=True), jnp.max(s_own, axis=-1, keepdims=True))
    p_sel = jnp.exp(s_sel - m)
    p_own = jnp.exp(s_own - m)
    denom = jnp.sum(p_sel, axis=-1, keepdims=True) + jnp.sum(p_own, axis=-1, keepdims=True)
    out = jnp.dot(p_own.astype(BF16), vn_ref[0].astype(BF16), preferred_element_type=F32)
    p_sel = p_sel.astype(BF16)
    for t in range(n_tok):
        v_t = vbuf[slot, t * per_tok:(t + 1) * per_tok].reshape(per_tok * PAGE_SIZE, HEAD_DIM).astype(BF16)
        out = out + jnp.where(row == t, jnp.dot(p_sel, v_t, preferred_element_type=F32), 0.0)
    o_ref[0] = (out / denom).astype(o_ref.dtype)


def _moba_sample_call(sel, page_table, q_pad, k_pad, v_pad, cache_k, cache_v, n_tok):
    db, n_pages = page_table.shape
    rows, w = q_pad.shape[1:]
    n_sel = n_tok * MOBA_TOPK * PAGES_PER_BLOCK
    new_spec = pl.BlockSpec((1, rows, HEAD_DIM), lambda s, *_: (s // N_HEADS, 0, s % N_HEADS))
    any_spec = pl.BlockSpec(memory_space=pl.ANY)
    return pl.pallas_call(
        functools.partial(_moba_sample_kernel, n_tok=n_tok, n_pages=n_pages),
        grid_spec=pltpu.PrefetchScalarGridSpec(
            num_scalar_prefetch=2,
            grid=(db * N_HEADS,),
            in_specs=[new_spec, new_spec, new_spec, any_spec, any_spec],
            out_specs=new_spec,
            scratch_shapes=[pltpu.VMEM((2, n_sel, PAGE_SIZE, HEAD_DIM), F32),
                            pltpu.VMEM((2, n_sel, PAGE_SIZE, HEAD_DIM), F32),
                            pltpu.SemaphoreType.DMA((2, 2))]),
        out_shape=jax.ShapeDtypeStruct((db, rows, w), BF16),
        compiler_params=_params("arbitrary"),
        name="moba_sample",
    )(sel.reshape(-1), page_table.reshape(-1), q_pad, k_pad, v_pad, cache_k, cache_v)


def _rope_tables(first, count, repeats=1):
    half = HEAD_DIM // 2
    inv = ROPE_THETA ** (-np.arange(half, dtype=np.float64) / half)
    ang = np.arange(first, first + count, dtype=np.float64)[:, None] * inv[None, :]
    cos, sin = np.cos(ang), np.sin(ang)
    cos_full = np.tile(np.concatenate([cos, cos], axis=-1), (repeats, 1))
    sin_signed = np.tile(np.concatenate([-sin, sin], axis=-1), (repeats, 1))
    return jnp.asarray(cos_full, F32), jnp.asarray(sin_signed, F32)


def _attn_inputs(x, mods, gain, w_in, q_gain, k_gain, cos, sin, tm):
    shift_a, scale_a = mods
    d = x.shape[1]
    h = _norm_call(x, gain, scale_a, shift_a, tm)
    w_qm, w_km, w_vm, w_qs, w_ks, w_vs = [(w_in, i * WIDTH, WIDTH) for i in range(6)]
    w_gm, w_gs = [(w_in, 6 * WIDTH + i * d, d) for i in range(2)]
    q_scale = ATTN_SCALE * LOG2E
    qm_f, qm_b = _proj_call("rope", h, w_qm, tm, out_scale=q_scale, gain=q_gain, cos=cos, sin=sin)
    km_f, km_b = _proj_call("rope", h, w_km, tm, gain=k_gain, cos=cos, sin=sin)
    vm_f, = _proj_call("raw", h, w_vm, tm, dtypes=(F32,))
    qs_b, = _proj_call("raw", h, w_qs, tm, out_scale=q_scale, dtypes=(BF16,))
    ks_f, ks_b = _proj_call("raw", h, w_ks, tm)
    vs_f, = _proj_call("raw", h, w_vs, tm, dtypes=(F32,))
    g_m = _proj_call("sigmoid", h, w_gm, tm)
    g_s = _proj_call("sigmoid", h, w_gs, tm)
    return dict(qm_f=qm_f, qm_b=qm_b, km_f=km_f, km_b=km_b, vm_f=vm_f,
                qs_b=qs_b, ks_f=ks_f, ks_b=ks_b, vs_f=vs_f, g_m=g_m, g_s=g_s)


def kernel(x_prompt, x_sample, cache_k_moba, cache_v_moba, cache_k_sb, cache_v_sb, page_table, c_prompt, c_sample, w_ada, b_ada, attn_norm_g, w_in, q_norm_g, k_norm_g, w_br_moba, w_br_sb, w_out, mlp_norm_g, w_ff1, w_ff2):
    batch, seq, d = x_prompt.shape
    db, n_tok, _ = x_sample.shape
    depth, n_phys = cache_k_moba.shape[:2]
    n_pages = page_table.shape[1]
    past_len = n_pages * PAGE_SIZE
    assert past_len % MOBA_BLOCK == 0 and past_len // MOBA_BLOCK >= MOBA_TOPK and n_tok <= SUBLANES
    m_p, m_s = batch * seq, db * n_tok
    tm_p, tm_s = 512, m_s
    assert m_s % SUBLANES == 0 and seq % tm_p == 0

    cos_p, sin_p = _rope_tables(0, seq)
    cos_s, sin_s = _rope_tables(past_len, n_tok, repeats=db)

    n_c = batch + db
    c_rows = -(-n_c // SUBLANES) * SUBLANES
    c_all = jnp.pad(jnp.concatenate([c_prompt, c_sample], axis=0), ((0, c_rows - n_c), (0, 0)))
    assert w_in.shape[2] == 6 * WIDTH + 2 * d and (6 * WIDTH) % d == 0
    cache_shape = (depth * n_phys, PAGE_SIZE, N_HEADS, HEAD_DIM)
    ck_m, cv_m = cache_k_moba.reshape(cache_shape), cache_v_moba.reshape(cache_shape)
    ck_s, cv_s = cache_k_sb.reshape(cache_shape), cache_v_sb.reshape(cache_shape)

    def pad_rows(v, rows):
        return jnp.pad(v.reshape(db, n_tok, WIDTH), ((0, 0), (0, rows - n_tok), (0, 0)))

    xp = x_prompt.reshape(m_p, d)
    xs = x_sample.reshape(m_s, d)
    outs = [[] for _ in range(8)]
    for l in range(depth):
        mods = _ada_mods(c_all, w_ada[l], b_ada[l])
        mods_p = [mods[:batch, i * d:(i + 1) * d].reshape(batch, 1, d) for i in range(N_MOD)]
        mods_s = [jnp.repeat(mods[batch:n_c, i * d:(i + 1) * d], n_tok, axis=0).reshape(1, m_s, d)
                  for i in range(N_MOD)]
        w_parts = w_in[l].astype(BF16)
        w_m, w_s, w_o = w_br_moba[l].astype(BF16), w_br_sb[l].astype(BF16), w_out[l].astype(BF16)
        w_1, w_2 = w_ff1[l].astype(BF16), w_ff2[l].astype(BF16)

        pt = page_table + l * n_phys
        a = _attn_inputs(xp, mods_p[:2], attn_norm_g[l], w_parts, q_norm_g[l], k_norm_g[l], cos_p, sin_p, tm_p)
        o_m, block_sums = _moba_prompt_call(a["qm_f"], a["qm_b"], a["km_f"], a["km_b"], a["vm_f"], batch, ck_m, pt)
        o_s = _sb_prompt_call(a["qs_b"], a["ks_b"], a["vs_f"], batch)
        x1, h2 = _merge_call(o_m, o_s, a["g_m"], a["g_s"], xp, w_m, w_s, w_o, mods_p[2], mlp_norm_g[l],
                             mods_p[4], mods_p[3], 256)
        xp = _ffn_call(h2, w_1, w_2, x1, mods_p[5], tm_p, 1024)
        for dst, key in zip(outs[:4], ("km_f", "vm_f", "ks_f", "vs_f")):
            dst.append(a[key].reshape(batch, seq, N_HEADS, HEAD_DIM))

        a = _attn_inputs(xs, mods_s[:2], attn_norm_g[l], w_parts, q_norm_g[l], k_norm_g[l], cos_s, sin_s, tm_s)
        qm_pad = pad_rows(a["qm_f"], SUBLANES)
        picks = _moba_select_call(block_sums, qm_pad)
        sel = picks.reshape(db, N_HEADS, SUBLANES, LANES)[:, :, :n_tok, :MOBA_TOPK]
        o_m = _moba_sample_call(sel, pt, qm_pad * ATTN_SCALE, pad_rows(a["km_f"], SUBLANES),
                                pad_rows(a["vm_f"], SUBLANES), ck_m, cv_m, n_tok)
        o_m = o_m[:, :n_tok].reshape(m_s, WIDTH)
        o_s = _sb_sample_call(pad_rows(a["qs_b"], SUBLANES).astype(F32), pad_rows(a["ks_b"], PAGE_SIZE),
                              pad_rows(a["vs_f"], PAGE_SIZE).astype(BF16), ck_s, cv_s, pt, n_tok)
        o_s = o_s[:, :n_tok].reshape(m_s, WIDTH)
        x1, h2 = _merge_call(o_m, o_s, a["g_m"], a["g_s"], xs, w_m, w_s, w_o, mods_s[2], mlp_norm_g[l],
                             mods_s[4], mods_s[3], tm_s)
        xs = _ffn_call(h2, w_1, w_2, x1, mods_s[5], tm_s, 1024)
        for dst, key in zip(outs[4:], ("km_f", "vm_f", "ks_f", "vs_f")):
            dst.append(a[key].reshape(db, n_tok, N_HEADS, HEAD_DIM))

    return (xp.reshape(batch, seq, d), xs.reshape(db, n_tok, d), *[jnp.stack(o) for o in outs])
```
